```python
import jax, jax.numpy as jnp
from jax import lax
import numpy as np

D_MODEL = 1024
BATCH = 8
SEQ = 2048
DEPTH = 1
DEC_BATCH = 8
DEC_SEQ = 8192
PAST_LEN = 128

GRID_W = 64
Q_BLOCK = 128
ROPE_THETA = 10000.0
NORM_EPS = 1e-6
GQA_HEADS = 8
GQA_KV_HEADS = 2
GQA_HEAD_DIM = 64
MLA_HEADS = 8
MLA_Q_RANK = 256
MLA_KV_RANK = 128
MLA_NOPE_DIM = 64
MLA_ROPE_DIM = 32
MLA_V_DIM = 64
GQA_WIDTH = GQA_HEADS * GQA_HEAD_DIM
MLA_WIDTH = MLA_HEADS * MLA_V_DIM
MIX_WIDTH = GQA_WIDTH + MLA_WIDTH
IN_SPLITS = (GQA_WIDTH, GQA_KV_HEADS * GQA_HEAD_DIM, GQA_KV_HEADS * GQA_HEAD_DIM, MLA_Q_RANK, MLA_KV_RANK, MLA_ROPE_DIM)
IN_WIDTH = sum(IN_SPLITS)
D_FF = -(-8 * D_MODEL // (3 * 256)) * 256

kernel_name = 'hymba_gqa_mla_deepnorm_adaln_encoder'


def _rms_norm(x, g):
    xf = x.astype(jnp.float32)
    y = xf * lax.rsqrt(jnp.mean(xf * xf, axis=-1, keepdims=True) + NORM_EPS)
    return (y * g.astype(jnp.float32)).astype(x.dtype)


def _layer_norm(x, g=None, b=None):
    xf = x.astype(jnp.float32)
    xc = xf - jnp.mean(xf, axis=-1, keepdims=True)
    y = xc * lax.rsqrt(jnp.mean(xc * xc, axis=-1, keepdims=True) + NORM_EPS)
    if g is not None:
        y = y * g.astype(jnp.float32) + b.astype(jnp.float32)
    return y.astype(x.dtype)


def _axial_rope(seq_len, rot_dim):
    rows = seq_len // GRID_W
    row = jnp.repeat(jnp.arange(rows, dtype=jnp.float32), GRID_W)
    col = jnp.tile(jnp.arange(GRID_W, dtype=jnp.float32), rows)
    n = rot_dim // 4
    inv_freq = 1.0 / jnp.power(ROPE_THETA, jnp.arange(n, dtype=jnp.float32) / n)
    ang = jnp.concatenate([row[:, None] * inv_freq, col[:, None] * inv_freq], axis=-1)
    return jnp.cos(ang), jnp.sin(ang)


def _apply_rope(x, cos, sin):
    xf = x.astype(jnp.float32)
    half = x.shape[-1] // 2
    x1, x2 = xf[..., :half], xf[..., half:]
    c = cos[None, :, None, :]
    s = sin[None, :, None, :]
    return jnp.concatenate([x1 * c - x2 * s, x1 * s + x2 * c], axis=-1).astype(x.dtype)


def _block_attention(q, k, v, scale):
    B, S, KV, G, dq = q.shape
    dv = v.shape[-1]
    nb = S // Q_BLOCK
    qb = q.reshape(B, nb, Q_BLOCK, KV, G, dq).transpose(1, 0, 2, 3, 4, 5)

    def one_block(q_blk):
        s = jnp.einsum('bqkgd,bskd->bkgqs', q_blk, k).astype(jnp.float32) * scale
        p = jax.nn.softmax(s, axis=-1).astype(v.dtype)
        return jnp.einsum('bkgqs,bskd->bqkgd', p, v)

    o = lax.map(one_block, qb)
    return o.transpose(1, 0, 2, 3, 4, 5).reshape(B, S, KV * G * dv)


def _encoder_layer(x, c, rope_a, rope_b, w_ada, b_ada, w_in, gqa_q_norm, gqa_k_norm,
                   mla_q_norm, w_uq, mla_kv_norm, w_ukv, grp_norm_a, grp_norm_b, w_out,
                   ln1_g, ln1_b, w_gate, w_up, w_down, ln2_g, ln2_b):
    B, S, _ = x.shape
    alpha = (2.0 * DEPTH) ** 0.25
    mod = jax.nn.silu(c) @ w_ada + b_ada
    sh1, sc1, g1, sh2, sc2, g2 = [m[:, None, :] for m in jnp.split(mod, 6, axis=-1)]

    u = _layer_norm(x) * (1 + sc1) + sh1
    h = u @ w_in
    offs = [int(o) for o in np.cumsum(IN_SPLITS)[:-1]]
    q_a, k_a, v_a, q_lat, kv_lat, k_rope = jnp.split(h, offs, axis=-1)

    q_a = _apply_rope(_rms_norm(q_a.reshape(B, S, GQA_HEADS, GQA_HEAD_DIM), gqa_q_norm), *rope_a)
    k_a = _apply_rope(_rms_norm(k_a.reshape(B, S, GQA_KV_HEADS, GQA_HEAD_DIM), gqa_k_norm), *rope_a)
    v_a = v_a.reshape(B, S, GQA_KV_HEADS, GQA_HEAD_DIM)
    q_a = q_a.reshape(B, S, GQA_KV_HEADS, GQA_HEADS // GQA_KV_HEADS, GQA_HEAD_DIM)
    o_a = _block_attention(q_a, k_a, v_a, GQA_HEAD_DIM ** -0.5)

    q_m = (_rms_norm(q_lat, mla_q_norm) @ w_uq).reshape(B, S, MLA_HEADS, MLA_NOPE_DIM + MLA_ROPE_DIM)
    q_nope = q_m[..., :MLA_NOPE_DIM]
    q_pe = _apply_rope(q_m[..., MLA_NOPE_DIM:], *rope_b)
    kv = (_rms_norm(kv_lat, mla_kv_norm) @ w_ukv).reshape(B, S, MLA_HEADS, MLA_NOPE_DIM + MLA_V_DIM)
    k_nope, v_m = kv[..., :MLA_NOPE_DIM], kv[..., MLA_NOPE_DIM:]
    k_pe = _apply_rope(k_rope[:, :, None, :], *rope_b)
    k_m = jnp.concatenate([k_nope, jnp.broadcast_to(k_pe, (B, S, MLA_HEADS, MLA_ROPE_DIM))], axis=-1)
    q_m = jnp.concatenate([q_nope, q_pe], axis=-1)[:, :, :, None, :]
    o_m = _block_attention(q_m, k_m, v_m, (MLA_NOPE_DIM + MLA_ROPE_DIM) ** -0.5)

    mix = jnp.concatenate([_rms_norm(o_a, grp_norm_a), _rms_norm(o_m, grp_norm_b)], axis=-1) @ w_out
    x = _layer_norm(alpha * x + g1 * mix, ln1_g, ln1_b)

    u2 = _layer_norm(x) * (1 + sc2) + sh2
    f = (jax.nn.silu(u2 @ w_gate) * (u2 @ w_up)) @ w_down
    return _layer_norm(alpha * x + g2 * f, ln2_g, ln2_b)


def _trunk(x, c, params):
    S = x.shape[1]
    rope_a = _axial_rope(S, GQA_HEAD_DIM)
    rope_b = _axial_rope(S, MLA_ROPE_DIM)
    for l in range(DEPTH):
        x = _encoder_layer(x, c, rope_a, rope_b, *[p[l] for p in params])
    return x


def setup_inputs(seed: int = 0) -> dict:
    key = jax.random.key(seed)
    ks = jax.random.split(key, 24)
    beta = (8.0 * DEPTH) ** -0.25

    def dense(k, shape, fan_in, mult=1.0):
        return jax.random.normal(k, shape, jnp.float32) * (fan_in ** -0.5) * mult

    def gain(k, n):
        return 1.0 + 0.1 * jax.random.normal(k, (DEPTH, n), jnp.float32)

    def bias(k, n):
        return 0.02 * jax.random.normal(k, (DEPTH, n), jnp.float32)

    return {
        'x_prompt': jax.random.normal(ks[0], (BATCH, SEQ, D_MODEL), jnp.float32),
        'x_sample': jax.random.normal(ks[1], (DEC_BATCH, DEC_SEQ, D_MODEL), jnp.float32),
        'c_prompt': jax.random.normal(ks[2], (BATCH, D_MODEL), jnp.float32),
        'c_sample': jax.random.normal(ks[3], (DEC_BATCH, D_MODEL), jnp.float32),
        'w_ada': dense(ks[4], (DEPTH, D_MODEL, 6 * D_MODEL), D_MODEL, 0.5),
        'b_ada': bias(ks[5], 6 * D_MODEL),
        'w_in': dense(ks[6], (DEPTH, D_MODEL, IN_WIDTH), D_MODEL),
        'gqa_q_norm': gain(ks[7], GQA_HEAD_DIM),
        'gqa_k_norm': gain(ks[8], GQA_HEAD_DIM),
        'mla_q_norm': gain(ks[9], MLA_Q_RANK),
        'w_uq': dense(ks[10], (DEPTH, MLA_Q_RANK, MLA_HEADS * (MLA_NOPE_DIM + MLA_ROPE_DIM)), MLA_Q_RANK),
        'mla_kv_norm': gain(ks[11], MLA_KV_RANK),
        'w_ukv': dense(ks[12], (DEPTH, MLA_KV_RANK, MLA_HEADS * (MLA_NOPE_DIM + MLA_V_DIM)), MLA_KV_RANK),
        'grp_norm_a': gain(ks[13], GQA_WIDTH),
        'grp_norm_b': gain(ks[14], MLA_WIDTH),
        'w_out': dense(ks[15], (DEPTH, MIX_WIDTH, D_MODEL), MIX_WIDTH, beta),
        'ln1_g': gain(ks[16], D_MODEL),
        'ln1_b': bias(ks[17], D_MODEL),
        'w_gate': dense(ks[18], (DEPTH, D_MODEL, D_FF), D_MODEL),
        'w_up': dense(ks[19], (DEPTH, D_MODEL, D_FF), D_MODEL),
        'w_down': dense(ks[20], (DEPTH, D_FF, D_MODEL), D_FF, beta),
        'ln2_g': gain(ks[21], D_MODEL),
        'ln2_b': bias(ks[22], D_MODEL),
    }


def reference(x_prompt, x_sample, c_prompt, c_sample, w_ada, b_ada, w_in, gqa_q_norm, gqa_k_norm,
              mla_q_norm, w_uq, mla_kv_norm, w_ukv, grp_norm_a, grp_norm_b, w_out,
              ln1_g, ln1_b, w_gate, w_up, w_down, ln2_g, ln2_b):
    params = (w_ada, b_ada, w_in, gqa_q_norm, gqa_k_norm, mla_q_norm, w_uq, mla_kv_norm, w_ukv,
              grp_norm_a, grp_norm_b, w_out, ln1_g, ln1_b, w_gate, w_up, w_down, ln2_g, ln2_b)
    y_prompt = _trunk(x_prompt, c_prompt, params)
    y_sample = _trunk(x_sample, c_sample, params)
    return (y_prompt, y_sample)
```

```python
import functools
import math

import jax
import jax.numpy as jnp
from jax import lax
from jax.experimental import pallas as pl
from jax.experimental.pallas import tpu as pltpu

D_MODEL = 1024
DEPTH = 1
GRID_W = 64
ROPE_THETA = 10000.0
NORM_EPS = 1e-6
GQA_HEADS = 8
GQA_KV_HEADS = 2
GQA_GROUP = GQA_HEADS // GQA_KV_HEADS
GQA_HEAD_DIM = 64
MLA_HEADS = 8
MLA_Q_RANK = 256
MLA_KV_RANK = 128
MLA_NOPE_DIM = 64
MLA_ROPE_DIM = 32
MLA_QK_DIM = MLA_NOPE_DIM + MLA_ROPE_DIM
MLA_V_DIM = 64
GQA_WIDTH = GQA_HEADS * GQA_HEAD_DIM
MLA_WIDTH = MLA_HEADS * MLA_V_DIM
MIX_WIDTH = GQA_WIDTH + MLA_WIDTH
GQA_KV_WIDTH = GQA_KV_HEADS * GQA_HEAD_DIM
IN_WIDTH = GQA_WIDTH + 2 * GQA_KV_WIDTH + MLA_Q_RANK + MLA_KV_RANK + MLA_ROPE_DIM
D_FF = -(-8 * D_MODEL // (3 * 256)) * 256
ALPHA = (2.0 * DEPTH) ** 0.25

OFF_QA = 0
OFF_KA = OFF_QA + GQA_WIDTH
OFF_VA = OFF_KA + GQA_KV_WIDTH
OFF_QL = OFF_VA + GQA_KV_WIDTH
OFF_KVL = OFF_QL + MLA_Q_RANK
OFF_KR = OFF_KVL + MLA_KV_RANK

QK_PAD = 128
HEAD_V = 64
LOG2E = math.log2(math.e)

TM_PRE = 512
TK = 512
TQ = 256
TM_POST = 512
FF_CHUNK = D_FF // 2
VMEM_LIMIT = 56 * 1024 * 1024

F32 = jnp.float32
BF16 = jnp.bfloat16


def _layer_norm_rows(x):
    mu = jnp.mean(x, axis=-1, keepdims=True)
    xc = x - mu
    var = jnp.mean(xc * xc, axis=-1, keepdims=True)
    return xc * lax.rsqrt(var + NORM_EPS)


def _rms_cols(xt, gain_col):
    ms = jnp.mean(xt * xt, axis=0, keepdims=True)
    return xt * lax.rsqrt(ms + NORM_EPS) * gain_col


def _rope_cols(xt, cos_t, sin_t):
    half = xt.shape[0] // 2
    x1, x2 = xt[:half], xt[half:]
    return jnp.concatenate([x1 * cos_t - x2 * sin_t, x1 * sin_t + x2 * cos_t], axis=0)


def _ada_kernel(c_ref, w_ref, b_ref, o_ref):
    c = c_ref[...]
    s = c * (1.0 / (1.0 + jnp.exp(-c)))
    o_ref[...] = jnp.dot(s, w_ref[...], preferred_element_type=F32) + b_ref[...]


def _ada_mod(c, w_ada, b_ada):
    n = c.shape[0]
    cols = 6 * D_MODEL
    return pl.pallas_call(
        _ada_kernel,
        grid=(6,),
        in_specs=[
            pl.BlockSpec((n, D_MODEL), lambda j: (0, 0)),
            pl.BlockSpec((D_MODEL, D_MODEL), lambda j: (0, j)),
            pl.BlockSpec((1, D_MODEL), lambda j: (0, j)),
        ],
        out_specs=pl.BlockSpec((n, D_MODEL), lambda j: (0, j)),
        out_shape=jax.ShapeDtypeStruct((n, cols), F32),
        name="ada_mod",
    )(c, w_ada, b_ada.reshape(1, cols))


def _pre_kernel(x_ref, mod_ref, w_in_ref, w_uq_ref, w_ukv_ref,
                gq_ref, gk_ref, gql_ref, gkvl_ref,
                cosa_ref, sina_ref, cosb_ref, sinb_ref,
                qa_ref, ka_ref, va_ref, qm_ref, km_ref, vm_ref, *, qscale_m):
    x = x_ref[0]
    mod = mod_ref[0]
    sh1, sc1 = mod[0:1], mod[1:2]
    u = (_layer_norm_rows(x) * (1.0 + sc1) + sh1).astype(BF16)
    ht = lax.dot_general(w_in_ref[...], u, (((1,), (1,)), ((), ())),
                         preferred_element_type=F32)
    tm = ht.shape[1]
    n_chunks = tm // TK
    cos_a, sin_a = cosa_ref[...], sina_ref[...]
    cos_b, sin_b = cosb_ref[...], sinb_ref[...]
    zeros_half = jnp.zeros((GQA_HEAD_DIM, tm), F32)

    gq = gq_ref[...]
    for h in range(GQA_HEADS):
        slab = ht[OFF_QA + h * GQA_HEAD_DIM: OFF_QA + (h + 1) * GQA_HEAD_DIM]
        q = _rope_cols(_rms_cols(slab, gq), cos_a, sin_a)
        parts = [zeros_half] * GQA_KV_HEADS
        parts[h // GQA_GROUP] = q
        qa_ref[0, h] = jnp.concatenate(parts, axis=0).astype(BF16)

    gk = gk_ref[...]
    k_parts = []
    for j in range(GQA_KV_HEADS):
        slab = ht[OFF_KA + j * GQA_HEAD_DIM: OFF_KA + (j + 1) * GQA_HEAD_DIM]
        k_parts.append(_rope_cols(_rms_cols(slab, gk), cos_a, sin_a))
        v = ht[OFF_VA + j * GQA_HEAD_DIM: OFF_VA + (j + 1) * GQA_HEAD_DIM].astype(BF16)
        for c in range(n_chunks):
            va_ref[0, j, c] = v[:, c * TK:(c + 1) * TK]
    ka_ref[0, 0] = jnp.concatenate(k_parts, axis=0).T.astype(BF16)

    ql = _rms_cols(ht[OFF_QL: OFF_QL + MLA_Q_RANK], gql_ref[...]).astype(BF16)
    qmt = jnp.dot(w_uq_ref[...], ql, preferred_element_type=F32)
    zeros_pad = jnp.zeros((QK_PAD - MLA_QK_DIM, tm), F32)
    for h in range(MLA_HEADS):
        base = h * MLA_QK_DIM
        nope = qmt[base: base + MLA_NOPE_DIM]
        pe = _rope_cols(qmt[base + MLA_NOPE_DIM: base + MLA_QK_DIM], cos_b, sin_b)
        q = jnp.concatenate([nope, pe], axis=0) * qscale_m
        qm_ref[0, h] = jnp.concatenate([q, zeros_pad], axis=0).astype(BF16)

    kvl = _rms_cols(ht[OFF_KVL: OFF_KVL + MLA_KV_RANK], gkvl_ref[...]).astype(BF16)
    kvt = jnp.dot(w_ukv_ref[...], kvl, preferred_element_type=F32)
    k_pe = _rope_cols(ht[OFF_KR: OFF_KR + MLA_ROPE_DIM], cos_b, sin_b)
    for h in range(MLA_HEADS):
        base = h * (MLA_NOPE_DIM + MLA_V_DIM)
        k_nope = kvt[base: base + MLA_NOPE_DIM]
        v = kvt[base + MLA_NOPE_DIM: base + MLA_NOPE_DIM + MLA_V_DIM].astype(BF16)
        for c in range(n_chunks):
            vm_ref[0, h, c] = v[:, c * TK:(c + 1) * TK]
        kt = jnp.concatenate([k_nope, k_pe, zeros_pad], axis=0)
        km_ref[0, h] = kt.T.astype(BF16)


def _pre_attention(x, mod, prep, ropes):
    B, S, _ = x.shape
    tm = TM_PRE
    nt = S // tm
    cpt = tm // TK
    nck = S // TK
    const = lambda shape: pl.BlockSpec(shape, lambda b, i: (0,) * len(shape))
    rope_a = pl.BlockSpec((GQA_HEAD_DIM // 2, tm), lambda b, i: (0, i))
    rope_b = pl.BlockSpec((MLA_ROPE_DIM // 2, tm), lambda b, i: (0, i))
    out_shape = (
        jax.ShapeDtypeStruct((B, GQA_HEADS, QK_PAD, S), BF16),
        jax.ShapeDtypeStruct((B, 1, S, QK_PAD), BF16),
        jax.ShapeDtypeStruct((B, GQA_KV_HEADS, nck, HEAD_V, TK), BF16),
        jax.ShapeDtypeStruct((B, MLA_HEADS, QK_PAD, S), BF16),
        jax.ShapeDtypeStruct((B, MLA_HEADS, S, QK_PAD), BF16),
        jax.ShapeDtypeStruct((B, MLA_HEADS, nck, HEAD_V, TK), BF16),
    )
    out_specs = (
        pl.BlockSpec((1, GQA_HEADS, QK_PAD, tm), lambda b, i: (b, 0, 0, i)),
        pl.BlockSpec((1, 1, tm, QK_PAD), lambda b, i: (b, 0, i, 0)),
        pl.BlockSpec((1, GQA_KV_HEADS, cpt, HEAD_V, TK), lambda b, i: (b, 0, i, 0, 0)),
        pl.BlockSpec((1, MLA_HEADS, QK_PAD, tm), lambda b, i: (b, 0, 0, i)),
        pl.BlockSpec((1, MLA_HEADS, tm, QK_PAD), lambda b, i: (b, 0, i, 0)),
        pl.BlockSpec((1, MLA_HEADS, cpt, HEAD_V, TK), lambda b, i: (b, 0, i, 0, 0)),
    )
    return pl.pallas_call(
        functools.partial(_pre_kernel, qscale_m=MLA_QK_DIM ** -0.5 * LOG2E),
        grid=(B, nt),
        in_specs=[
            pl.BlockSpec((1, tm, D_MODEL), lambda b, i: (b, i, 0)),
            pl.BlockSpec((1, 6, D_MODEL), lambda b, i: (b, 0, 0)),
            const((IN_WIDTH, D_MODEL)),
            const((MLA_HEADS * MLA_QK_DIM, MLA_Q_RANK)),
            const((MLA_HEADS * (MLA_NOPE_DIM + MLA_V_DIM), MLA_KV_RANK)),
            const((GQA_HEAD_DIM, 1)),
            const((GQA_HEAD_DIM, 1)),
            const((MLA_Q_RANK, 1)),
            const((MLA_KV_RANK, 1)),
            rope_a, rope_a, rope_b, rope_b,
        ],
        out_specs=out_specs,
        out_shape=out_shape,
        compiler_params=pltpu.CompilerParams(
            dimension_semantics=("parallel", "parallel"),
            vmem_limit_bytes=VMEM_LIMIT),
        name="pre_attention",
    )(x, mod, prep["w_in_t"], prep["w_uq_t"], prep["w_ukv_t"],
      prep["gq_col"], prep["gk_col"], prep["gql_col"], prep["gkvl_col"],
      ropes["cos_a"], ropes["sin_a"], ropes["cos_b"], ropes["sin_b"])


def _attn_kernel(q_ref, k_ref, v_ref, o_ref, *, group, n_chunks):
    tq = q_ref.shape[-1]
    for g in range(group):
        q = q_ref[0, 0, g]

        def body(c, carry):
            m, l, acc = carry
            off = pl.multiple_of(c * TK, TK)
            kc = k_ref[0, 0, pl.ds(off, TK), :]
            vc = v_ref[0, 0, c]
            s = jnp.dot(kc, q, preferred_element_type=F32)
            m_new = jnp.maximum(m, jnp.max(s, axis=0, keepdims=True))
            p = jnp.exp2(s - m_new)
            alpha = jnp.exp2(m - m_new)
            l = alpha * l + jnp.sum(p, axis=0, keepdims=True)
            acc = alpha * acc + jnp.dot(vc, p.astype(BF16), preferred_element_type=F32)
            return m_new, l, acc

        init = (jnp.full((1, tq), -jnp.inf, F32), jnp.zeros((1, tq), F32),
                jnp.zeros((HEAD_V, tq), F32))
        _, l, acc = lax.fori_loop(0, n_chunks, body, init)
        o_ref[0, 0, g] = acc * (1.0 / l)


def _attention(q_t, k, v_t, group):
    B, heads, _, S = q_t.shape
    kv = heads // group
    kh = k.shape[1]
    nq = S // TQ
    q5 = q_t.reshape(B, kv, group, QK_PAD, S)
    k_map = (lambda b, j, i: (b, j, 0, 0)) if kh == kv else (lambda b, j, i: (b, 0, 0, 0))
    out = pl.pallas_call(
        functools.partial(_attn_kernel, group=group, n_chunks=S // TK),
        grid=(B, kv, nq),
        in_specs=[
            pl.BlockSpec((1, 1, group, QK_PAD, TQ), lambda b, j, i: (b, j, 0, 0, i)),
            pl.BlockSpec((1, 1, S, QK_PAD), k_map),
            pl.BlockSpec((1, 1, S // TK, HEAD_V, TK), lambda b, j, i: (b, j, 0, 0, 0)),
        ],
        out_specs=pl.BlockSpec((1, 1, group, HEAD_V, TQ), lambda b, j, i: (b, j, 0, 0, i)),
        out_shape=jax.ShapeDtypeStruct((B, kv, group, HEAD_V, S), F32),
        compiler_params=pltpu.CompilerParams(
            dimension_semantics=("parallel", "parallel", "parallel"),
            vmem_limit_bytes=VMEM_LIMIT),
        name="attention_g%d" % group,
    )(q5, k, v_t)
    return out.reshape(B, heads * HEAD_V, S)


def _post_kernel(x_ref, oa_ref, om_ref, mod_ref, ga_ref, gb_ref, w_out_ref,
                 ln1g_ref, ln1b_ref, w_gate_ref, w_up_ref, w_down_ref,
                 ln2g_ref, ln2b_ref, y_ref):
    x = x_ref[0]
    mod = mod_ref[0]
    g1, sh2, sc2, g2 = mod[2:3], mod[3:4], mod[4:5], mod[5:6]
    na = _rms_cols(oa_ref[0], ga_ref[...])
    nb = _rms_cols(om_ref[0], gb_ref[...])
    cat_t = jnp.concatenate([na, nb], axis=0).astype(BF16)
    mix = lax.dot_general(cat_t, w_out_ref[...], (((0,), (0,)), ((), ())),
                          preferred_element_type=F32)
    x1 = _layer_norm_rows(ALPHA * x + g1 * mix) * ln1g_ref[...] + ln1b_ref[...]
    u2 = (_layer_norm_rows(x1) * (1.0 + sc2) + sh2).astype(BF16)
    f = None
    for j in range(D_FF // FF_CHUNK):
        cols = slice(j * FF_CHUNK, (j + 1) * FF_CHUNK)
        gate = jnp.dot(u2, w_gate_ref[:, cols], preferred_element_type=F32)
        up = jnp.dot(u2, w_up_ref[:, cols], preferred_element_type=F32)
        act = (gate * (1.0 / (1.0 + jnp.exp(-gate))) * up).astype(BF16)
        part = jnp.dot(act, w_down_ref[cols, :], preferred_element_type=F32)
        f = part if f is None else f + part
    y_ref[0] = _layer_norm_rows(ALPHA * x1 + g2 * f) * ln2g_ref[...] + ln2b_ref[...]


def _post_attention(x, o_a, o_m, mod, prep):
    B, S, _ = x.shape
    tm = TM_POST
    single = pl.Buffered(1)
    const = lambda shape: pl.BlockSpec(shape, lambda b, i: (0,) * len(shape),
                                       pipeline_mode=single)
    return pl.pallas_call(
        _post_kernel,
        grid=(B, S // tm),
        in_specs=[
            pl.BlockSpec((1, tm, D_MODEL), lambda b, i: (b, i, 0)),
            pl.BlockSpec((1, GQA_WIDTH, tm), lambda b, i: (b, 0, i)),
            pl.BlockSpec((1, MLA_WIDTH, tm), lambda b, i: (b, 0, i)),
            pl.BlockSpec((1, 6, D_MODEL), lambda b, i: (b, 0, 0)),
            const((GQA_WIDTH, 1)),
            const((MLA_WIDTH, 1)),
            const((MIX_WIDTH, D_MODEL)),
            const((1, D_MODEL)),
            const((1, D_MODEL)),
            const((D_MODEL, D_FF)),
            const((D_MODEL, D_FF)),
            const((D_FF, D_MODEL)),
            const((1, D_MODEL)),
            const((1, D_MODEL)),
        ],
        out_specs=pl.BlockSpec((1, tm, D_MODEL), lambda b, i: (b, i, 0)),
        out_shape=jax.ShapeDtypeStruct((B, S, D_MODEL), F32),
        compiler_params=pltpu.CompilerParams(
            dimension_semantics=("parallel", "parallel"),
            vmem_limit_bytes=VMEM_LIMIT),
        name="post_attention",
    )(x, o_a, o_m, mod, prep["ga_col"], prep["gb_col"], prep["w_out"],
      prep["ln1_g"], prep["ln1_b"], prep["w_gate"], prep["w_up"], prep["w_down"],
      prep["ln2_g"], prep["ln2_b"])


def _rope_tables(seq_len, rot_dim):
    rows = seq_len // GRID_W
    row = jnp.repeat(jnp.arange(rows, dtype=F32), GRID_W)
    col = jnp.tile(jnp.arange(GRID_W, dtype=F32), rows)
    n = rot_dim // 4
    inv_freq = 1.0 / jnp.power(ROPE_THETA, jnp.arange(n, dtype=F32) / n)
    ang = jnp.concatenate([inv_freq[:, None] * row[None, :],
                           inv_freq[:, None] * col[None, :]], axis=0)
    return jnp.cos(ang), jnp.sin(ang)


def _trunk(x, mod, prep):
    S = x.shape[1]
    cos_a, sin_a = _rope_tables(S, GQA_HEAD_DIM)
    cos_b, sin_b = _rope_tables(S, MLA_ROPE_DIM)
    ropes = dict(cos_a=cos_a, sin_a=sin_a, cos_b=cos_b, sin_b=sin_b)
    qa_t, k_a, va_t, qm_t, k_m, vm_t = _pre_attention(x, mod, prep, ropes)
    o_a = _attention(qa_t, k_a, va_t, GQA_GROUP)
    o_m = _attention(qm_t, k_m, vm_t, 1)
    return _post_attention(x, o_a, o_m, mod, prep)


def kernel(x_prompt, x_sample, c_prompt, c_sample, w_ada, b_ada, w_in, gqa_q_norm, gqa_k_norm,
           mla_q_norm, w_uq, mla_kv_norm, w_ukv, grp_norm_a, grp_norm_b, w_out,
           ln1_g, ln1_b, w_gate, w_up, w_down, ln2_g, ln2_b):
    l = 0
    col = lambda v: v.reshape(-1, 1).astype(F32)
    row = lambda v: v.reshape(1, -1).astype(F32)
    prep = dict(
        w_in_t=w_in[l].T.astype(BF16),
        w_uq_t=w_uq[l].T.astype(BF16),
        w_ukv_t=w_ukv[l].T.astype(BF16),
        gq_col=col(gqa_q_norm[l] * (GQA_HEAD_DIM ** -0.5 * LOG2E)),
        gk_col=col(gqa_k_norm[l]),
        gql_col=col(mla_q_norm[l]),
        gkvl_col=col(mla_kv_norm[l]),
        ga_col=col(grp_norm_a[l]),
        gb_col=col(grp_norm_b[l]),
        w_out=w_out[l].astype(BF16),
        ln1_g=row(ln1_g[l]), ln1_b=row(ln1_b[l]),
        w_gate=w_gate[l].astype(BF16),
        w_up=w_up[l].astype(BF16),
        w_down=w_down[l].astype(BF16),
        ln2_g=row(ln2_g[l]), ln2_b=row(ln2_b[l]),
    )
    nb = c_prompt.shape[0]
    mod = _ada_mod(jnp.concatenate([c_prompt, c_sample], axis=0), w_ada[l], b_ada[l])
    mod = mod.reshape(-1, 6, D_MODEL)
    y_prompt = _trunk(x_prompt, mod[:nb], prep)
    y_sample = _trunk(x_sample, mod[nb:], prep)
    return (y_prompt, y_sample)
```

```python
import functools
import math

import jax
import jax.numpy as jnp
from jax import lax
from jax.experimental import pallas as pl
from jax.experimental.pallas import tpu as pltpu

D_MODEL = 1024
DEPTH = 1
GRID_W = 64
ROPE_THETA = 10000.0
NORM_EPS = 1e-6
GQA_HEADS = 8
GQA_KV_HEADS = 2
GQA_GROUP = GQA_HEADS // GQA_KV_HEADS
GQA_HEAD_DIM = 64
MLA_HEADS = 8
MLA_Q_RANK = 256
MLA_KV_RANK = 128
MLA_NOPE_DIM = 64
MLA_ROPE_DIM = 32
MLA_QK_DIM = MLA_NOPE_DIM + MLA_ROPE_DIM
MLA_V_DIM = 64
GQA_WIDTH = GQA_HEADS * GQA_HEAD_DIM
MLA_WIDTH = MLA_HEADS * MLA_V_DIM
MIX_WIDTH = GQA_WIDTH + MLA_WIDTH
GQA_KV_WIDTH = GQA_KV_HEADS * GQA_HEAD_DIM
IN_WIDTH = GQA_WIDTH + 2 * GQA_KV_WIDTH + MLA_Q_RANK + MLA_KV_RANK + MLA_ROPE_DIM
D_FF = -(-8 * D_MODEL // (3 * 256)) * 256
ALPHA = (2.0 * DEPTH) ** 0.25

OFF_QA = 0
OFF_KA = OFF_QA + GQA_WIDTH
OFF_VA = OFF_KA + GQA_KV_WIDTH
OFF_QL = OFF_VA + GQA_KV_WIDTH
OFF_KVL = OFF_QL + MLA_Q_RANK
OFF_KR = OFF_KVL + MLA_KV_RANK

QK_PAD = 128
HEAD_V = 64
LOG2E = math.log2(math.e)

TM_PRE = 512
TK = 512
N_STREAMS = GQA_GROUP
STREAM_W = 256
TM_POST = 512
FF_CHUNK = D_FF // 2
VMEM_LIMIT = 56 * 1024 * 1024

F32 = jnp.float32
BF16 = jnp.bfloat16


def _layer_norm_rows(x):
    mu = jnp.mean(x, axis=-1, keepdims=True)
    xc = x - mu
    var = jnp.mean(xc * xc, axis=-1, keepdims=True)
    return xc * lax.rsqrt(var + NORM_EPS)


def _rms_cols(xt, gain_col):
    ms = jnp.mean(xt * xt, axis=0, keepdims=True)
    return xt * lax.rsqrt(ms + NORM_EPS) * gain_col


def _rope_cols(xt, cos_t, sin_t):
    half = xt.shape[0] // 2
    x1, x2 = xt[:half], xt[half:]
    return jnp.concatenate([x1 * cos_t - x2 * sin_t, x1 * sin_t + x2 * cos_t], axis=0)


def _ada_kernel(c_ref, w_ref, b_ref, o_ref):
    c = c_ref[...]
    s = c * (1.0 / (1.0 + jnp.exp(-c)))
    o_ref[...] = jnp.dot(s, w_ref[...], preferred_element_type=F32) + b_ref[...]


def _ada_mod(c, w_ada, b_ada):
    n = c.shape[0]
    cols = 6 * D_MODEL
    return pl.pallas_call(
        _ada_kernel,
        grid=(6,),
        in_specs=[
            pl.BlockSpec((n, D_MODEL), lambda j: (0, 0)),
            pl.BlockSpec((D_MODEL, D_MODEL), lambda j: (0, j)),
            pl.BlockSpec((1, D_MODEL), lambda j: (0, j)),
        ],
        out_specs=pl.BlockSpec((n, D_MODEL), lambda j: (0, j)),
        out_shape=jax.ShapeDtypeStruct((n, cols), F32),
        name="ada_mod",
    )(c, w_ada, b_ada.reshape(1, cols))


def _pre_kernel(x_ref, mod_ref, w_in_ref, w_uq_ref, w_ukv_ref,
                gq_ref, gk_ref, gql_ref, gkvl_ref,
                cosa_ref, sina_ref, cosb_ref, sinb_ref,
                qa_ref, ka_ref, va_ref, qm_ref, km_ref, vm_ref, *, qscale_m):
    x = x_ref[0]
    mod = mod_ref[0]
    sh1, sc1 = mod[0:1], mod[1:2]
    u = (_layer_norm_rows(x) * (1.0 + sc1) + sh1).astype(BF16)
    ht = lax.dot_general(w_in_ref[...], u, (((1,), (1,)), ((), ())),
                         preferred_element_type=F32)
    tm = ht.shape[1]
    n_chunks = tm // TK
    cos_a, sin_a = cosa_ref[...], sina_ref[...]
    cos_b, sin_b = cosb_ref[...], sinb_ref[...]
    zeros_half = jnp.zeros((GQA_HEAD_DIM, tm), F32)

    gq = gq_ref[...]
    for h in range(GQA_HEADS):
        slab = ht[OFF_QA + h * GQA_HEAD_DIM: OFF_QA + (h + 1) * GQA_HEAD_DIM]
        q = _rope_cols(_rms_cols(slab, gq), cos_a, sin_a)
        parts = [zeros_half] * GQA_KV_HEADS
        parts[h // GQA_GROUP] = q
        qa_ref[0, h] = jnp.concatenate(parts, axis=0).astype(BF16)

    gk = gk_ref[...]
    k_parts = []
    for j in range(GQA_KV_HEADS):
        slab = ht[OFF_KA + j * GQA_HEAD_DIM: OFF_KA + (j + 1) * GQA_HEAD_DIM]
        k_parts.append(_rope_cols(_rms_cols(slab, gk), cos_a, sin_a))
        v = ht[OFF_VA + j * GQA_HEAD_DIM: OFF_VA + (j + 1) * GQA_HEAD_DIM].astype(BF16)
        for c in range(n_chunks):
            va_ref[0, j, c] = v[:, c * TK:(c + 1) * TK]
    ka_ref[0, 0] = jnp.concatenate(k_parts, axis=0).T.astype(BF16)

    ql = _rms_cols(ht[OFF_QL: OFF_QL + MLA_Q_RANK], gql_ref[...]).astype(BF16)
    qmt = jnp.dot(w_uq_ref[...], ql, preferred_element_type=F32)
    zeros_pad = jnp.zeros((QK_PAD - MLA_QK_DIM, tm), F32)
    for h in range(MLA_HEADS):
        base = h * MLA_QK_DIM
        nope = qmt[base: base + MLA_NOPE_DIM]
        pe = _rope_cols(qmt[base + MLA_NOPE_DIM: base + MLA_QK_DIM], cos_b, sin_b)
        q = jnp.concatenate([nope, pe], axis=0) * qscale_m
        qm_ref[0, h] = jnp.concatenate([q, zeros_pad], axis=0).astype(BF16)

    kvl = _rms_cols(ht[OFF_KVL: OFF_KVL + MLA_KV_RANK], gkvl_ref[...]).astype(BF16)
    kvt = jnp.dot(w_ukv_ref[...], kvl, preferred_element_type=F32)
    k_pe = _rope_cols(ht[OFF_KR: OFF_KR + MLA_ROPE_DIM], cos_b, sin_b)
    for h in range(MLA_HEADS):
        base = h * (MLA_NOPE_DIM + MLA_V_DIM)
        k_nope = kvt[base: base + MLA_NOPE_DIM]
        v = kvt[base + MLA_NOPE_DIM: base + MLA_NOPE_DIM + MLA_V_DIM].astype(BF16)
        for c in range(n_chunks):
            vm_ref[0, h, c] = v[:, c * TK:(c + 1) * TK]
        kt = jnp.concatenate([k_nope, k_pe, zeros_pad], axis=0)
        km_ref[0, h] = kt.T.astype(BF16)


def _pre_attention(x, mod, prep, ropes):
    B, S, _ = x.shape
    tm = TM_PRE
    nt = S // tm
    cpt = tm // TK
    nck = S // TK
    const = lambda shape: pl.BlockSpec(shape, lambda b, i: (0,) * len(shape))
    rope_a = pl.BlockSpec((GQA_HEAD_DIM // 2, tm), lambda b, i: (0, i))
    rope_b = pl.BlockSpec((MLA_ROPE_DIM // 2, tm), lambda b, i: (0, i))
    out_shape = (
        jax.ShapeDtypeStruct((B, GQA_HEADS, QK_PAD, S), BF16),
        jax.ShapeDtypeStruct((B, 1, S, QK_PAD), BF16),
        jax.ShapeDtypeStruct((B, GQA_KV_HEADS, nck, HEAD_V, TK), BF16),
        jax.ShapeDtypeStruct((B, MLA_HEADS, QK_PAD, S), BF16),
        jax.ShapeDtypeStruct((B, MLA_HEADS, S, QK_PAD), BF16),
        jax.ShapeDtypeStruct((B, MLA_HEADS, nck, HEAD_V, TK), BF16),
    )
    out_specs = (
        pl.BlockSpec((1, GQA_HEADS, QK_PAD, tm), lambda b, i: (b, 0, 0, i)),
        pl.BlockSpec((1, 1, tm, QK_PAD), lambda b, i: (b, 0, i, 0)),
        pl.BlockSpec((1, GQA_KV_HEADS, cpt, HEAD_V, TK), lambda b, i: (b, 0, i, 0, 0)),
        pl.BlockSpec((1, MLA_HEADS, QK_PAD, tm), lambda b, i: (b, 0, 0, i)),
        pl.BlockSpec((1, MLA_HEADS, tm, QK_PAD), lambda b, i: (b, 0, i, 0)),
        pl.BlockSpec((1, MLA_HEADS, cpt, HEAD_V, TK), lambda b, i: (b, 0, i, 0, 0)),
    )
    return pl.pallas_call(
        functools.partial(_pre_kernel, qscale_m=MLA_QK_DIM ** -0.5 * LOG2E),
        grid=(B, nt),
        in_specs=[
            pl.BlockSpec((1, tm, D_MODEL), lambda b, i: (b, i, 0)),
            pl.BlockSpec((1, 6, D_MODEL), lambda b, i: (b, 0, 0)),
            const((IN_WIDTH, D_MODEL)),
            const((MLA_HEADS * MLA_QK_DIM, MLA_Q_RANK)),
            const((MLA_HEADS * (MLA_NOPE_DIM + MLA_V_DIM), MLA_KV_RANK)),
            const((GQA_HEAD_DIM, 1)),
            const((GQA_HEAD_DIM, 1)),
            const((MLA_Q_RANK, 1)),
            const((MLA_KV_RANK, 1)),
            rope_a, rope_a, rope_b, rope_b,
        ],
        out_specs=out_specs,
        out_shape=out_shape,
        compiler_params=pltpu.CompilerParams(
            dimension_semantics=("parallel", "parallel"),
            vmem_limit_bytes=VMEM_LIMIT),
        name="pre_attention",
    )(x, mod, prep["w_in_t"], prep["w_uq_t"], prep["w_ukv_t"],
      prep["gq_col"], prep["gk_col"], prep["gql_col"], prep["gkvl_col"],
      ropes["cos_a"], ropes["sin_a"], ropes["cos_b"], ropes["sin_b"])


def _attn_kernel(q_ref, k_ref, v_ref, o_ref, s0_ref, s1_ref, acc_ref, *, grouped, n_chunks):
    def q_stream(i):
        if grouped:
            return q_ref[0, 0, i]
        return q_ref[0, 0, 0, :, i * STREAM_W:(i + 1) * STREAM_W]

    def scores(c, s_ref):
        off = pl.multiple_of(c * TK, TK)
        kc = k_ref[0, 0, pl.ds(off, TK), :]
        for i in range(N_STREAMS):
            s_ref[i] = jnp.dot(kc, q_stream(i), preferred_element_type=F32)

    def update(c, s_ref, ms, ls):
        vc = v_ref[0, 0, c]
        new_ms, new_ls = [], []
        for i in range(N_STREAMS):
            s = s_ref[i]
            m_new = jnp.maximum(ms[i], jnp.max(s, axis=0, keepdims=True))
            p = jnp.exp2(s - m_new)
            alpha = jnp.exp2(ms[i] - m_new)
            new_ls.append(alpha * ls[i] + jnp.sum(p, axis=0, keepdims=True))
            pv = jnp.dot(vc, p.astype(BF16), preferred_element_type=F32)
            acc_ref[i] = alpha * acc_ref[i] + pv
            new_ms.append(m_new)
        return tuple(new_ms), tuple(new_ls)

    acc_ref[...] = jnp.zeros(acc_ref.shape, F32)
    ms = tuple(jnp.full((1, STREAM_W), -jnp.inf, F32) for _ in range(N_STREAMS))
    ls = tuple(jnp.zeros((1, STREAM_W), F32) for _ in range(N_STREAMS))
    scores(0, s0_ref)

    def pair(j, carry):
        ms, ls = carry
        c = 2 * j
        scores(c + 1, s1_ref)
        ms, ls = update(c, s0_ref, ms, ls)
        scores(c + 2, s0_ref)
        ms, ls = update(c + 1, s1_ref, ms, ls)
        return ms, ls

    ms, ls = lax.fori_loop(0, n_chunks // 2 - 1, pair, (ms, ls))
    c = n_chunks - 2
    scores(c + 1, s1_ref)
    ms, ls = update(c, s0_ref, ms, ls)
    ms, ls = update(c + 1, s1_ref, ms, ls)
    for i in range(N_STREAMS):
        o = acc_ref[i] * (1.0 / ls[i])
        if grouped:
            o_ref[0, 0, i] = o
        else:
            o_ref[0, 0, 0, :, i * STREAM_W:(i + 1) * STREAM_W] = o


def _attention(q_t, k, v_t, grouped):
    B, heads, _, S = q_t.shape
    n_chunks = S // TK
    assert n_chunks % 2 == 0
    if grouped:
        kv = heads // N_STREAMS
        q5 = q_t.reshape(B, kv, N_STREAMS, QK_PAD, S)
        q_spec = pl.BlockSpec((1, 1, N_STREAMS, QK_PAD, STREAM_W), lambda b, j, i: (b, j, 0, 0, i))
        k_map = lambda b, j, i: (b, 0, 0, 0)
        o_shape = (B, kv, N_STREAMS, HEAD_V, S)
        o_spec = pl.BlockSpec((1, 1, N_STREAMS, HEAD_V, STREAM_W), lambda b, j, i: (b, j, 0, 0, i))
        nq = S // STREAM_W
    else:
        kv = heads
        tq = N_STREAMS * STREAM_W
        q5 = q_t.reshape(B, kv, 1, QK_PAD, S)
        q_spec = pl.BlockSpec((1, 1, 1, QK_PAD, tq), lambda b, j, i: (b, j, 0, 0, i))
        k_map = lambda b, j, i: (b, j, 0, 0)
        o_shape = (B, kv, 1, HEAD_V, S)
        o_spec = pl.BlockSpec((1, 1, 1, HEAD_V, tq), lambda b, j, i: (b, j, 0, 0, i))
        nq = S // tq
    out = pl.pallas_call(
        functools.partial(_attn_kernel, grouped=grouped, n_chunks=n_chunks),
        grid=(B, kv, nq),
        in_specs=[
            q_spec,
            pl.BlockSpec((1, 1, S, QK_PAD), k_map),
            pl.BlockSpec((1, 1, n_chunks, HEAD_V, TK), lambda b, j, i: (b, j, 0, 0, 0)),
        ],
        out_specs=o_spec,
        out_shape=jax.ShapeDtypeStruct(o_shape, F32),
        scratch_shapes=[
            pltpu.VMEM((N_STREAMS, TK, STREAM_W), F32),
            pltpu.VMEM((N_STREAMS, TK, STREAM_W), F32),
            pltpu.VMEM((N_STREAMS, HEAD_V, STREAM_W), F32),
        ],
        compiler_params=pltpu.CompilerParams(
            dimension_semantics=("parallel", "parallel", "parallel"),
            vmem_limit_bytes=VMEM_LIMIT),
        name="attention_grouped" if grouped else "attention_single",
    )(q5, k, v_t)
    return out.reshape(B, heads * HEAD_V, S)


def _post_kernel(x_ref, oa_ref, om_ref, mod_ref, ga_ref, gb_ref, w_out_ref,
                 ln1g_ref, ln1b_ref, w_gate_ref, w_up_ref, w_down_ref,
                 ln2g_ref, ln2b_ref, y_ref):
    x = x_ref[0]
    mod = mod_ref[0]
    g1, sh2, sc2, g2 = mod[2:3], mod[3:4], mod[4:5], mod[5:6]
    na = _rms_cols(oa_ref[0], ga_ref[...])
    nb = _rms_cols(om_ref[0], gb_ref[...])
    cat_t = jnp.concatenate([na, nb], axis=0).astype(BF16)
    mix = lax.dot_general(cat_t, w_out_ref[...], (((0,), (0,)), ((), ())),
                          preferred_element_type=F32)
    x1 = _layer_norm_rows(ALPHA * x + g1 * mix) * ln1g_ref[...] + ln1b_ref[...]
    u2 = (_layer_norm_rows(x1) * (1.0 + sc2) + sh2).astype(BF16)
    f = None
    for j in range(D_FF // FF_CHUNK):
        cols = slice(j * FF_CHUNK, (j + 1) * FF_CHUNK)
        gate = jnp.dot(u2, w_gate_ref[:, cols], preferred_element_type=F32)
        up = jnp.dot(u2, w_up_ref[:, cols], preferred_element_type=F32)
        act = (gate * (1.0 / (1.0 + jnp.exp(-gate))) * up).astype(BF16)
        part = jnp.dot(act, w_down_ref[cols, :], preferred_element_type=F32)
        f = part if f is None else f + part
    y_ref[0] = _layer_norm_rows(ALPHA * x1 + g2 * f) * ln2g_ref[...] + ln2b_ref[...]


def _post_attention(x, o_a, o_m, mod, prep):
    B, S, _ = x.shape
    tm = TM_POST
    single = pl.Buffered(1)
    const = lambda shape: pl.BlockSpec(shape, lambda b, i: (0,) * len(shape),
                                       pipeline_mode=single)
    return pl.pallas_call(
        _post_kernel,
        grid=(B, S // tm),
        in_specs=[
            pl.BlockSpec((1, tm, D_MODEL), lambda b, i: (b, i, 0)),
            pl.BlockSpec((1, GQA_WIDTH, tm), lambda b, i: (b, 0, i)),
            pl.BlockSpec((1, MLA_WIDTH, tm), lambda b, i: (b, 0, i)),
            pl.BlockSpec((1, 6, D_MODEL), lambda b, i: (b, 0, 0)),
            const((GQA_WIDTH, 1)),
            const((MLA_WIDTH, 1)),
            const((MIX_WIDTH, D_MODEL)),
            const((1, D_MODEL)),
            const((1, D_MODEL)),
            const((D_MODEL, D_FF)),
            const((D_MODEL, D_FF)),
            const((D_FF, D_MODEL)),
            const((1, D_MODEL)),
            const((1, D_MODEL)),
        ],
        out_specs=pl.BlockSpec((1, tm, D_MODEL), lambda b, i: (b, i, 0)),
        out_shape=jax.ShapeDtypeStruct((B, S, D_MODEL), F32),
        compiler_params=pltpu.CompilerParams(
            dimension_semantics=("parallel", "parallel"),
            vmem_limit_bytes=VMEM_LIMIT),
        name="post_attention",
    )(x, o_a, o_m, mod, prep["ga_col"], prep["gb_col"], prep["w_out"],
      prep["ln1_g"], prep["ln1_b"], prep["w_gate"], prep["w_up"], prep["w_down"],
      prep["ln2_g"], prep["ln2_b"])


def _rope_tables(seq_len, rot_dim):
    rows = seq_len // GRID_W
    row = jnp.repeat(jnp.arange(rows, dtype=F32), GRID_W)
    col = jnp.tile(jnp.arange(GRID_W, dtype=F32), rows)
    n = rot_dim // 4
    inv_freq = 1.0 / jnp.power(ROPE_THETA, jnp.arange(n, dtype=F32) / n)
    ang = jnp.concatenate([inv_freq[:, None] * row[None, :],
                           inv_freq[:, None] * col[None, :]], axis=0)
    return jnp.cos(ang), jnp.sin(ang)


def _trunk(x, mod, prep):
    S = x.shape[1]
    cos_a, sin_a = _rope_tables(S, GQA_HEAD_DIM)
    cos_b, sin_b = _rope_tables(S, MLA_ROPE_DIM)
    ropes = dict(cos_a=cos_a, sin_a=sin_a, cos_b=cos_b, sin_b=sin_b)
    qa_t, k_a, va_t, qm_t, k_m, vm_t = _pre_attention(x, mod, prep, ropes)
    o_a = _attention(qa_t, k_a, va_t, True)
    o_m = _attention(qm_t, k_m, vm_t, False)
    return _post_attention(x, o_a, o_m, mod, prep)


def kernel(x_prompt, x_sample, c_prompt, c_sample, w_ada, b_ada, w_in, gqa_q_norm, gqa_k_norm,
           mla_q_norm, w_uq, mla_kv_norm, w_ukv, grp_norm_a, grp_norm_b, w_out,
           ln1_g, ln1_b, w_gate, w_up, w_down, ln2_g, ln2_b):
    l = 0
    col = lambda v: v.reshape(-1, 1).astype(F32)
    row = lambda v: v.reshape(1, -1).astype(F32)
    prep = dict(
        w_in_t=w_in[l].T.astype(BF16),
        w_uq_t=w_uq[l].T.astype(BF16),
        w_ukv_t=w_ukv[l].T.astype(BF16),
        gq_col=col(gqa_q_norm[l] * (GQA_HEAD_DIM ** -0.5 * LOG2E)),
        gk_col=col(gqa_k_norm[l]),
        gql_col=col(mla_q_norm[l]),
        gkvl_col=col(mla_kv_norm[l]),
        ga_col=col(grp_norm_a[l]),
        gb_col=col(grp_norm_b[l]),
        w_out=w_out[l].astype(BF16),
        ln1_g=row(ln1_g[l]), ln1_b=row(ln1_b[l]),
        w_gate=w_gate[l].astype(BF16),
        w_up=w_up[l].astype(BF16),
        w_down=w_down[l].astype(BF16),
        ln2_g=row(ln2_g[l]), ln2_b=row(ln2_b[l]),
    )
    nb = c_prompt.shape[0]
    mod = _ada_mod(jnp.concatenate([c_prompt, c_sample], axis=0), w_ada[l], b_ada[l])
    mod = mod.reshape(-1, 6, D_MODEL)
    y_prompt = _trunk(x_prompt, mod[:nb], prep)
    y_sample = _trunk(x_sample, mod[nb:], prep)
    return (y_prompt, y_sample)
```

```python
import functools
import math

import jax
import jax.numpy as jnp
from jax import lax
from jax.experimental import pallas as pl
from jax.experimental.pallas import tpu as pltpu

D_MODEL = 1024
DEPTH = 1
GRID_W = 64
ROPE_THETA = 10000.0
NORM_EPS = 1e-6
GQA_HEADS = 8
GQA_KV_HEADS = 2
GQA_GROUP = GQA_HEADS // GQA_KV_HEADS
GQA_HEAD_DIM = 64
MLA_HEADS = 8
MLA_Q_RANK = 256
MLA_KV_RANK = 128
MLA_NOPE_DIM = 64
MLA_ROPE_DIM = 32
MLA_QK_DIM = MLA_NOPE_DIM + MLA_ROPE_DIM
MLA_V_DIM = 64
GQA_WIDTH = GQA_HEADS * GQA_HEAD_DIM
MLA_WIDTH = MLA_HEADS * MLA_V_DIM
MIX_WIDTH = GQA_WIDTH + MLA_WIDTH
GQA_KV_WIDTH = GQA_KV_HEADS * GQA_HEAD_DIM
IN_WIDTH = GQA_WIDTH + 2 * GQA_KV_WIDTH + MLA_Q_RANK + MLA_KV_RANK + MLA_ROPE_DIM
D_FF = -(-8 * D_MODEL // (3 * 256)) * 256
ALPHA = (2.0 * DEPTH) ** 0.25

OFF_QA = 0
OFF_KA = OFF_QA + GQA_WIDTH
OFF_VA = OFF_KA + GQA_KV_WIDTH
OFF_QL = OFF_VA + GQA_KV_WIDTH
OFF_KVL = OFF_QL + MLA_Q_RANK
OFF_KR = OFF_KVL + MLA_KV_RANK

QK_PAD = 128
HEAD_V = 64
ONES_ROWS = 16
HEAD_V_EXT = HEAD_V + ONES_ROWS
LOG2E = math.log2(math.e)

TM_PRE = 512
TK = 512
CHUNKS_PER_TRIP = 4
N_STREAMS = GQA_GROUP
STREAM_W = 256
TM_POST = 512
MXU_TILE = 256
FF_BOUNDS = (0, (D_FF // MXU_TILE + 1) // 2 * MXU_TILE, D_FF)
VMEM_LIMIT = 56 * 1024 * 1024

F32 = jnp.float32
BF16 = jnp.bfloat16


def _layer_norm_rows(x):
    mu = jnp.mean(x, axis=-1, keepdims=True)
    xc = x - mu
    var = jnp.mean(xc * xc, axis=-1, keepdims=True)
    return xc * lax.rsqrt(var + NORM_EPS)


def _rms_cols(xt, gain_col):
    ms = jnp.mean(xt * xt, axis=0, keepdims=True)
    return xt * lax.rsqrt(ms + NORM_EPS) * gain_col


def _rope_cols(xt, cos_t, sin_t):
    half = xt.shape[0] // 2
    x1, x2 = xt[:half], xt[half:]
    return jnp.concatenate([x1 * cos_t - x2 * sin_t, x1 * sin_t + x2 * cos_t], axis=0)


def _ada_kernel(c_ref, w_ref, b_ref, o_ref):
    c = c_ref[...]
    s = c * (1.0 / (1.0 + jnp.exp(-c)))
    o_ref[...] = jnp.dot(s, w_ref[...], preferred_element_type=F32) + b_ref[...]


def _ada_mod(c, w_ada, b_ada):
    n = c.shape[0]
    cols = 6 * D_MODEL
    return pl.pallas_call(
        _ada_kernel,
        grid=(6,),
        in_specs=[
            pl.BlockSpec((n, D_MODEL), lambda j: (0, 0)),
            pl.BlockSpec((D_MODEL, D_MODEL), lambda j: (0, j)),
            pl.BlockSpec((1, D_MODEL), lambda j: (0, j)),
        ],
        out_specs=pl.BlockSpec((n, D_MODEL), lambda j: (0, j)),
        out_shape=jax.ShapeDtypeStruct((n, cols), F32),
        name="ada_mod",
    )(c, w_ada, b_ada.reshape(1, cols))


def _pre_kernel(x_ref, mod_ref, w_in_ref, w_uq_ref, w_ukv_ref,
                gq_ref, gk_ref, gql_ref, gkvl_ref,
                cosa_ref, sina_ref, cosb_ref, sinb_ref,
                qa_ref, ka_ref, va_ref, qm_ref, km_ref, vm_ref, *, qscale_m):
    x = x_ref[0]
    mod = mod_ref[0]
    sh1, sc1 = mod[0:1], mod[1:2]
    u = (_layer_norm_rows(x) * (1.0 + sc1) + sh1).astype(BF16)
    ht = lax.dot_general(w_in_ref[...], u, (((1,), (1,)), ((), ())),
                         preferred_element_type=F32)
    tm = ht.shape[1]
    n_chunks = tm // TK
    cos_a, sin_a = cosa_ref[...], sina_ref[...]
    cos_b, sin_b = cosb_ref[...], sinb_ref[...]
    zeros_half = jnp.zeros((GQA_HEAD_DIM, tm), F32)
    ones_rows = jnp.where(lax.broadcasted_iota(jnp.int32, (ONES_ROWS, tm), 0) == 0, 1.0, 0.0)

    gq = gq_ref[...]
    for h in range(GQA_HEADS):
        slab = ht[OFF_QA + h * GQA_HEAD_DIM: OFF_QA + (h + 1) * GQA_HEAD_DIM]
        q = _rope_cols(_rms_cols(slab, gq), cos_a, sin_a)
        parts = [zeros_half] * GQA_KV_HEADS
        parts[h // GQA_GROUP] = q
        qa_ref[0, h] = jnp.concatenate(parts, axis=0).astype(BF16)

    gk = gk_ref[...]
    k_parts = []
    for j in range(GQA_KV_HEADS):
        slab = ht[OFF_KA + j * GQA_HEAD_DIM: OFF_KA + (j + 1) * GQA_HEAD_DIM]
        k_parts.append(_rope_cols(_rms_cols(slab, gk), cos_a, sin_a))
        v = ht[OFF_VA + j * GQA_HEAD_DIM: OFF_VA + (j + 1) * GQA_HEAD_DIM]
        v = jnp.concatenate([v, ones_rows], axis=0).astype(BF16)
        for c in range(n_chunks):
            va_ref[0, j, c] = v[:, c * TK:(c + 1) * TK]
    ka_ref[0, 0] = jnp.concatenate(k_parts, axis=0).T.astype(BF16)

    ql = _rms_cols(ht[OFF_QL: OFF_QL + MLA_Q_RANK], gql_ref[...]).astype(BF16)
    qmt = jnp.dot(w_uq_ref[...], ql, preferred_element_type=F32)
    zeros_pad = jnp.zeros((QK_PAD - MLA_QK_DIM, tm), F32)
    for h in range(MLA_HEADS):
        base = h * MLA_QK_DIM
        nope = qmt[base: base + MLA_NOPE_DIM]
        pe = _rope_cols(qmt[base + MLA_NOPE_DIM: base + MLA_QK_DIM], cos_b, sin_b)
        q = jnp.concatenate([nope, pe], axis=0) * qscale_m
        qm_ref[0, h] = jnp.concatenate([q, zeros_pad], axis=0).astype(BF16)

    kvl = _rms_cols(ht[OFF_KVL: OFF_KVL + MLA_KV_RANK], gkvl_ref[...]).astype(BF16)
    kvt = jnp.dot(w_ukv_ref[...], kvl, preferred_element_type=F32)
    k_pe = _rope_cols(ht[OFF_KR: OFF_KR + MLA_ROPE_DIM], cos_b, sin_b)
    for h in range(MLA_HEADS):
        base = h * (MLA_NOPE_DIM + MLA_V_DIM)
        k_nope = kvt[base: base + MLA_NOPE_DIM]
        v = kvt[base + MLA_NOPE_DIM: base + MLA_NOPE_DIM + MLA_V_DIM]
        v = jnp.concatenate([v, ones_rows], axis=0).astype(BF16)
        for c in range(n_chunks):
            vm_ref[0, h, c] = v[:, c * TK:(c + 1) * TK]
        kt = jnp.concatenate([k_nope, k_pe, zeros_pad], axis=0)
        km_ref[0, h] = kt.T.astype(BF16)


def _pre_attention(x, mod, prep, ropes):
    B, S, _ = x.shape
    tm = TM_PRE
    nt = S // tm
    cpt = tm // TK
    nck = S // TK
    const = lambda shape: pl.BlockSpec(shape, lambda b, i: (0,) * len(shape))
    rope_a = pl.BlockSpec((GQA_HEAD_DIM // 2, tm), lambda b, i: (0, i))
    rope_b = pl.BlockSpec((MLA_ROPE_DIM // 2, tm), lambda b, i: (0, i))
    out_shape = (
        jax.ShapeDtypeStruct((B, GQA_HEADS, QK_PAD, S), BF16),
        jax.ShapeDtypeStruct((B, 1, S, QK_PAD), BF16),
        jax.ShapeDtypeStruct((B, GQA_KV_HEADS, nck, HEAD_V_EXT, TK), BF16),
        jax.ShapeDtypeStruct((B, MLA_HEADS, QK_PAD, S), BF16),
        jax.ShapeDtypeStruct((B, MLA_HEADS, S, QK_PAD), BF16),
        jax.ShapeDtypeStruct((B, MLA_HEADS, nck, HEAD_V_EXT, TK), BF16),
    )
    out_specs = (
        pl.BlockSpec((1, GQA_HEADS, QK_PAD, tm), lambda b, i: (b, 0, 0, i)),
        pl.BlockSpec((1, 1, tm, QK_PAD), lambda b, i: (b, 0, i, 0)),
        pl.BlockSpec((1, GQA_KV_HEADS, cpt, HEAD_V_EXT, TK), lambda b, i: (b, 0, i, 0, 0)),
        pl.BlockSpec((1, MLA_HEADS, QK_PAD, tm), lambda b, i: (b, 0, 0, i)),
        pl.BlockSpec((1, MLA_HEADS, tm, QK_PAD), lambda b, i: (b, 0, i, 0)),
        pl.BlockSpec((1, MLA_HEADS, cpt, HEAD_V_EXT, TK), lambda b, i: (b, 0, i, 0, 0)),
    )
    return pl.pallas_call(
        functools.partial(_pre_kernel, qscale_m=MLA_QK_DIM ** -0.5 * LOG2E),
        grid=(B, nt),
        in_specs=[
            pl.BlockSpec((1, tm, D_MODEL), lambda b, i: (b, i, 0)),
            pl.BlockSpec((1, 6, D_MODEL), lambda b, i: (b, 0, 0)),
            const((IN_WIDTH, D_MODEL)),
            const((MLA_HEADS * MLA_QK_DIM, MLA_Q_RANK)),
            const((MLA_HEADS * (MLA_NOPE_DIM + MLA_V_DIM), MLA_KV_RANK)),
            const((GQA_HEAD_DIM, 1)),
            const((GQA_HEAD_DIM, 1)),
            const((MLA_Q_RANK, 1)),
            const((MLA_KV_RANK, 1)),
            rope_a, rope_a, rope_b, rope_b,
        ],
        out_specs=out_specs,
        out_shape=out_shape,
        compiler_params=pltpu.CompilerParams(
            dimension_semantics=("parallel", "parallel"),
            vmem_limit_bytes=VMEM_LIMIT),
        name="pre_attention",
    )(x, mod, prep["w_in_t"], prep["w_uq_t"], prep["w_ukv_t"],
      prep["gq_col"], prep["gk_col"], prep["gql_col"], prep["gkvl_col"],
      ropes["cos_a"], ropes["sin_a"], ropes["cos_b"], ropes["sin_b"])


def _attn_kernel(q_ref, k_ref, v_ref, o_ref, s0_ref, s1_ref, cmax0_ref, cmax1_ref, acc_ref,
                 *, grouped, n_chunks):
    def q_stream(i):
        if grouped:
            return q_ref[0, 0, i]
        return q_ref[0, 0, 0, :, i * STREAM_W:(i + 1) * STREAM_W]

    def scores(c, slot):
        s_ref, cmax_ref = slot
        off = pl.multiple_of(c * TK, TK)
        kc = k_ref[0, 0, pl.ds(off, TK), :]
        for i in range(N_STREAMS):
            s = jnp.dot(kc, q_stream(i), preferred_element_type=F32)
            s_ref[i] = s
            cmax_ref[i] = jnp.max(s, axis=0, keepdims=True)

    def update(c, slot, ms):
        s_ref, cmax_ref = slot
        vc = v_ref[0, 0, c]
        new_ms = []
        for i in range(N_STREAMS):
            m_new = jnp.maximum(ms[i], cmax_ref[i])
            p = jnp.exp2(s_ref[i] - m_new).astype(BF16)
            alpha = jnp.exp2(ms[i] - m_new)
            acc_ref[i] = alpha * acc_ref[i] + jnp.dot(vc, p, preferred_element_type=F32)
            new_ms.append(m_new)
        return tuple(new_ms)

    slots = ((s0_ref, cmax0_ref), (s1_ref, cmax1_ref))

    def trip(c0, ms, prefetch_next_trip):
        for u in range(CHUNKS_PER_TRIP):
            if u + 1 < CHUNKS_PER_TRIP or prefetch_next_trip:
                scores(c0 + u + 1, slots[(u + 1) % 2])
            ms = update(c0 + u, slots[u % 2], ms)
        return ms

    acc_ref[...] = jnp.zeros(acc_ref.shape, F32)
    ms = tuple(jnp.full((1, STREAM_W), -jnp.inf, F32) for _ in range(N_STREAMS))
    scores(0, slots[0])
    ms = lax.fori_loop(0, n_chunks // CHUNKS_PER_TRIP - 1,
                       lambda t, ms: trip(t * CHUNKS_PER_TRIP, ms, True), ms)
    trip(n_chunks - CHUNKS_PER_TRIP, ms, False)
    for i in range(N_STREAMS):
        acc = acc_ref[i]
        o = acc[:HEAD_V] * (1.0 / acc[HEAD_V:HEAD_V + 1])
        if grouped:
            o_ref[0, 0, i] = o
        else:
            o_ref[0, 0, 0, :, i * STREAM_W:(i + 1) * STREAM_W] = o


def _attention(q_t, k, v_t, grouped):
    B, heads, _, S = q_t.shape
    n_chunks = S // TK
    assert n_chunks % CHUNKS_PER_TRIP == 0
    if grouped:
        kv = heads // N_STREAMS
        q5 = q_t.reshape(B, kv, N_STREAMS, QK_PAD, S)
        q_spec = pl.BlockSpec((1, 1, N_STREAMS, QK_PAD, STREAM_W), lambda b, j, i: (b, j, 0, 0, i))
        k_map = lambda b, j, i: (b, 0, 0, 0)
        o_shape = (B, kv, N_STREAMS, HEAD_V, S)
        o_spec = pl.BlockSpec((1, 1, N_STREAMS, HEAD_V, STREAM_W), lambda b, j, i: (b, j, 0, 0, i))
        nq = S // STREAM_W
    else:
        kv = heads
        tq = N_STREAMS * STREAM_W
        q5 = q_t.reshape(B, kv, 1, QK_PAD, S)
        q_spec = pl.BlockSpec((1, 1, 1, QK_PAD, tq), lambda b, j, i: (b, j, 0, 0, i))
        k_map = lambda b, j, i: (b, j, 0, 0)
        o_shape = (B, kv, 1, HEAD_V, S)
        o_spec = pl.BlockSpec((1, 1, 1, HEAD_V, tq), lambda b, j, i: (b, j, 0, 0, i))
        nq = S // tq
    out = pl.pallas_call(
        functools.partial(_attn_kernel, grouped=grouped, n_chunks=n_chunks),
        grid=(B, kv, nq),
        in_specs=[
            q_spec,
            pl.BlockSpec((1, 1, S, QK_PAD), k_map),
            pl.BlockSpec((1, 1, n_chunks, HEAD_V_EXT, TK), lambda b, j, i: (b, j, 0, 0, 0)),
        ],
        out_specs=o_spec,
        out_shape=jax.ShapeDtypeStruct(o_shape, F32),
        scratch_shapes=[
            pltpu.VMEM((N_STREAMS, TK, STREAM_W), F32),
            pltpu.VMEM((N_STREAMS, TK, STREAM_W), F32),
            pltpu.VMEM((N_STREAMS, 1, STREAM_W), F32),
            pltpu.VMEM((N_STREAMS, 1, STREAM_W), F32),
            pltpu.VMEM((N_STREAMS, HEAD_V_EXT, STREAM_W), F32),
        ],
        compiler_params=pltpu.CompilerParams(
            dimension_semantics=("parallel", "parallel", "parallel"),
            vmem_limit_bytes=VMEM_LIMIT),
        name="attention_grouped" if grouped else "attention_single",
    )(q5, k, v_t)
    return out.reshape(B, heads * HEAD_V, S)


def _post_kernel(x_ref, oa_ref, om_ref, mod_ref, ga_ref, gb_ref, w_out_ref,
                 ln1g_ref, ln1b_ref, w_gate_ref, w_up_ref, w_down_ref,
                 ln2g_ref, ln2b_ref, y_ref):
    x = x_ref[0]
    mod = mod_ref[0]
    g1, sh2, sc2, g2 = mod[2:3], mod[3:4], mod[4:5], mod[5:6]
    na = _rms_cols(oa_ref[0], ga_ref[...])
    nb = _rms_cols(om_ref[0], gb_ref[...])
    cat_t = jnp.concatenate([na, nb], axis=0).astype(BF16)
    mix = lax.dot_general(cat_t, w_out_ref[...], (((0,), (0,)), ((), ())),
                          preferred_element_type=F32)
    x1 = _layer_norm_rows(ALPHA * x + g1 * mix) * ln1g_ref[...] + ln1b_ref[...]
    u2 = (_layer_norm_rows(x1) * (1.0 + sc2) + sh2).astype(BF16)
    f = None
    for lo, hi in zip(FF_BOUNDS[:-1], FF_BOUNDS[1:]):
        cols = slice(lo, hi)
        gate = jnp.dot(u2, w_gate_ref[:, cols], preferred_element_type=F32)
        up = jnp.dot(u2, w_up_ref[:, cols], preferred_element_type=F32)
        act = (gate * (1.0 / (1.0 + jnp.exp(-gate))) * up).astype(BF16)
        part = jnp.dot(act, w_down_ref[cols, :], preferred_element_type=F32)
        f = part if f is None else f + part
    y_ref[0] = _layer_norm_rows(ALPHA * x1 + g2 * f) * ln2g_ref[...] + ln2b_ref[...]


def _post_attention(x, o_a, o_m, mod, prep):
    B, S, _ = x.shape
    tm = TM_POST
    single = pl.Buffered(1)
    const = lambda shape: pl.BlockSpec(shape, lambda b, i: (0,) * len(shape),
                                       pipeline_mode=single)
    return pl.pallas_call(
        _post_kernel,
        grid=(B, S // tm),
        in_specs=[
            pl.BlockSpec((1, tm, D_MODEL), lambda b, i: (b, i, 0)),
            pl.BlockSpec((1, GQA_WIDTH, tm), lambda b, i: (b, 0, i)),
            pl.BlockSpec((1, MLA_WIDTH, tm), lambda b, i: (b, 0, i)),
            pl.BlockSpec((1, 6, D_MODEL), lambda b, i: (b, 0, 0)),
            const((GQA_WIDTH, 1)),
            const((MLA_WIDTH, 1)),
            const((MIX_WIDTH, D_MODEL)),
            const((1, D_MODEL)),
            const((1, D_MODEL)),
            const((D_MODEL, D_FF)),
            const((D_MODEL, D_FF)),
            const((D_FF, D_MODEL)),
            const((1, D_MODEL)),
            const((1, D_MODEL)),
        ],
        out_specs=pl.BlockSpec((1, tm, D_MODEL), lambda b, i: (b, i, 0)),
        out_shape=jax.ShapeDtypeStruct((B, S, D_MODEL), F32),
        compiler_params=pltpu.CompilerParams(
            dimension_semantics=("parallel", "parallel"),
            vmem_limit_bytes=VMEM_LIMIT),
        name="post_attention",
    )(x, o_a, o_m, mod, prep["ga_col"], prep["gb_col"], prep["w_out"],
      prep["ln1_g"], prep["ln1_b"], prep["w_gate"], prep["w_up"], prep["w_down"],
      prep["ln2_g"], prep["ln2_b"])


def _rope_tables(seq_len, rot_dim):
    rows = seq_len // GRID_W
    row = jnp.repeat(jnp.arange(rows, dtype=F32), GRID_W)
    col = jnp.tile(jnp.arange(GRID_W, dtype=F32), rows)
    n = rot_dim // 4
    inv_freq = 1.0 / jnp.power(ROPE_THETA, jnp.arange(n, dtype=F32) / n)
    ang = jnp.concatenate([inv_freq[:, None] * row[None, :],
                           inv_freq[:, None] * col[None, :]], axis=0)
    return jnp.cos(ang), jnp.sin(ang)


def _trunk(x, mod, prep):
    S = x.shape[1]
    cos_a, sin_a = _rope_tables(S, GQA_HEAD_DIM)
    cos_b, sin_b = _rope_tables(S, MLA_ROPE_DIM)
    ropes = dict(cos_a=cos_a, sin_a=sin_a, cos_b=cos_b, sin_b=sin_b)
    qa_t, k_a, va_t, qm_t, k_m, vm_t = _pre_attention(x, mod, prep, ropes)
    o_a = _attention(qa_t, k_a, va_t, True)
    o_m = _attention(qm_t, k_m, vm_t, False)
    return _post_attention(x, o_a, o_m, mod, prep)


def kernel(x_prompt, x_sample, c_prompt, c_sample, w_ada, b_ada, w_in, gqa_q_norm, gqa_k_norm,
           mla_q_norm, w_uq, mla_kv_norm, w_ukv, grp_norm_a, grp_norm_b, w_out,
           ln1_g, ln1_b, w_gate, w_up, w_down, ln2_g, ln2_b):
    l = 0
    col = lambda v: v.reshape(-1, 1).astype(F32)
    row = lambda v: v.reshape(1, -1).astype(F32)
    prep = dict(
        w_in_t=w_in[l].T.astype(BF16),
        w_uq_t=w_uq[l].T.astype(BF16),
        w_ukv_t=w_ukv[l].T.astype(BF16),
        gq_col=col(gqa_q_norm[l] * (GQA_HEAD_DIM ** -0.5 * LOG2E)),
        gk_col=col(gqa_k_norm[l]),
        gql_col=col(mla_q_norm[l]),
        gkvl_col=col(mla_kv_norm[l]),
        ga_col=col(grp_norm_a[l]),
        gb_col=col(grp_norm_b[l]),
        w_out=w_out[l].astype(BF16),
        ln1_g=row(ln1_g[l]), ln1_b=row(ln1_b[l]),
        w_gate=w_gate[l].astype(BF16),
        w_up=w_up[l].astype(BF16),
        w_down=w_down[l].astype(BF16),
        ln2_g=row(ln2_g[l]), ln2_b=row(ln2_b[l]),
    )
    nb = c_prompt.shape[0]
    mod = _ada_mod(jnp.concatenate([c_prompt, c_sample], axis=0), w_ada[l], b_ada[l])
    mod = mod.reshape(-1, 6, D_MODEL)
    y_prompt = _trunk(x_prompt, mod[:nb], prep)
    y_sample = _trunk(x_sample, mod[nb:], prep)
    return (y_prompt, y_sample)
```

```python
import functools
import math

import jax
import jax.numpy as jnp
from jax import lax
from jax.experimental import pallas as pl
from jax.experimental.pallas import tpu as pltpu

D_MODEL = 1024
DEPTH = 1
GRID_W = 64
ROPE_THETA = 10000.0
NORM_EPS = 1e-6
GQA_HEADS = 8
GQA_KV_HEADS = 2
GQA_GROUP = GQA_HEADS // GQA_KV_HEADS
GQA_HEAD_DIM = 64
MLA_HEADS = 8
MLA_Q_RANK = 256
MLA_KV_RANK = 128
MLA_NOPE_DIM = 64
MLA_ROPE_DIM = 32
MLA_QK_DIM = MLA_NOPE_DIM + MLA_ROPE_DIM
MLA_V_DIM = 64
GQA_WIDTH = GQA_HEADS * GQA_HEAD_DIM
MLA_WIDTH = MLA_HEADS * MLA_V_DIM
MIX_WIDTH = GQA_WIDTH + MLA_WIDTH
GQA_KV_WIDTH = GQA_KV_HEADS * GQA_HEAD_DIM
IN_WIDTH = GQA_WIDTH + 2 * GQA_KV_WIDTH + MLA_Q_RANK + MLA_KV_RANK + MLA_ROPE_DIM
D_FF = -(-8 * D_MODEL // (3 * 256)) * 256
ALPHA = (2.0 * DEPTH) ** 0.25

OFF_QA = 0
OFF_KA = OFF_QA + GQA_WIDTH
OFF_VA = OFF_KA + GQA_KV_WIDTH
OFF_QL = OFF_VA + GQA_KV_WIDTH
OFF_KVL = OFF_QL + MLA_Q_RANK
OFF_KR = OFF_KVL + MLA_KV_RANK

QK_PAD = 128
HEAD_V = 64
ONES_ROWS = 16
HEAD_V_EXT = HEAD_V + ONES_ROWS
LOG2E = math.log2(math.e)

TM_PRE = 512
TK = 512
CHUNKS_PER_TRIP = 4
N_STREAMS = GQA_GROUP
STREAM_W = 256
TM_POST = 512
MXU_TILE = 256
FF_BOUNDS = (0, (D_FF // MXU_TILE + 1) // 2 * MXU_TILE, D_FF)
VMEM_LIMIT = 56 * 1024 * 1024

F32 = jnp.float32
BF16 = jnp.bfloat16


def _layer_norm_rows(x):
    mu = jnp.mean(x, axis=-1, keepdims=True)
    xc = x - mu
    var = jnp.mean(xc * xc, axis=-1, keepdims=True)
    return xc * lax.rsqrt(var + NORM_EPS)


def _rms_cols(xt, gain_col):
    ms = jnp.mean(xt * xt, axis=0, keepdims=True)
    return xt * lax.rsqrt(ms + NORM_EPS) * gain_col


def _rope_cols(xt, cos_t, sin_t):
    half = xt.shape[0] // 2
    x1, x2 = xt[:half], xt[half:]
    return jnp.concatenate([x1 * cos_t - x2 * sin_t, x1 * sin_t + x2 * cos_t], axis=0)


def _ada_kernel(c_ref, w_ref, b_ref, o_ref):
    c = c_ref[...]
    s = c * (1.0 / (1.0 + jnp.exp(-c)))
    o_ref[...] = jnp.dot(s, w_ref[...], preferred_element_type=F32) + b_ref[...]


def _ada_mod(c, w_ada, b_ada):
    n = c.shape[0]
    cols = 6 * D_MODEL
    return pl.pallas_call(
        _ada_kernel,
        grid=(6,),
        in_specs=[
            pl.BlockSpec((n, D_MODEL), lambda j: (0, 0)),
            pl.BlockSpec((D_MODEL, D_MODEL), lambda j: (0, j)),
            pl.BlockSpec((1, D_MODEL), lambda j: (0, j)),
        ],
        out_specs=pl.BlockSpec((n, D_MODEL), lambda j: (0, j)),
        out_shape=jax.ShapeDtypeStruct((n, cols), F32),
        name="ada_mod",
    )(c, w_ada, b_ada.reshape(1, cols))


def _pre_kernel(x_ref, mod_ref, w_in_ref, w_uq_ref, w_ukv_ref,
                gq_ref, gk_ref, gql_ref, gkvl_ref,
                cosa_ref, sina_ref, cosb_ref, sinb_ref,
                qa_ref, ka_ref, va_ref, qm_ref, km_ref, vm_ref, *, qscale_m):
    x = x_ref[0]
    mod = mod_ref[0]
    sh1, sc1 = mod[0:1], mod[1:2]
    u = (_layer_norm_rows(x) * (1.0 + sc1) + sh1).astype(BF16)
    ht = lax.dot_general(w_in_ref[...], u, (((1,), (1,)), ((), ())),
                         preferred_element_type=F32)
    tm = ht.shape[1]
    n_chunks = tm // TK
    cos_a, sin_a = cosa_ref[...], sina_ref[...]
    cos_b, sin_b = cosb_ref[...], sinb_ref[...]
    zeros_half = jnp.zeros((GQA_HEAD_DIM, tm), F32)
    ones_rows = jnp.where(lax.broadcasted_iota(jnp.int32, (ONES_ROWS, tm), 0) == 0, 1.0, 0.0)

    gq = gq_ref[...]
    for h in range(GQA_HEADS):
        slab = ht[OFF_QA + h * GQA_HEAD_DIM: OFF_QA + (h + 1) * GQA_HEAD_DIM]
        q = _rope_cols(_rms_cols(slab, gq), cos_a, sin_a)
        parts = [zeros_half] * GQA_KV_HEADS
        parts[h // GQA_GROUP] = q
        qa_ref[0, h] = jnp.concatenate(parts, axis=0).astype(BF16)

    gk = gk_ref[...]
    k_parts = []
    for j in range(GQA_KV_HEADS):
        slab = ht[OFF_KA + j * GQA_HEAD_DIM: OFF_KA + (j + 1) * GQA_HEAD_DIM]
        k_parts.append(_rope_cols(_rms_cols(slab, gk), cos_a, sin_a))
        v = ht[OFF_VA + j * GQA_HEAD_DIM: OFF_VA + (j + 1) * GQA_HEAD_DIM]
        v = jnp.concatenate([v, ones_rows], axis=0).astype(BF16)
        for c in range(n_chunks):
            va_ref[0, j, c] = v[:, c * TK:(c + 1) * TK]
    ka_ref[0, 0] = jnp.concatenate(k_parts, axis=0).T.astype(BF16)

    ql = _rms_cols(ht[OFF_QL: OFF_QL + MLA_Q_RANK], gql_ref[...]).astype(BF16)
    qmt = jnp.dot(w_uq_ref[...], ql, preferred_element_type=F32)
    zeros_pad = jnp.zeros((QK_PAD - MLA_QK_DIM, tm), F32)
    for h in range(MLA_HEADS):
        base = h * MLA_QK_DIM
        nope = qmt[base: base + MLA_NOPE_DIM]
        pe = _rope_cols(qmt[base + MLA_NOPE_DIM: base + MLA_QK_DIM], cos_b, sin_b)
        q = jnp.concatenate([nope, pe], axis=0) * qscale_m
        qm_ref[0, h] = jnp.concatenate([q, zeros_pad], axis=0).astype(BF16)

    kvl = _rms_cols(ht[OFF_KVL: OFF_KVL + MLA_KV_RANK], gkvl_ref[...]).astype(BF16)
    kvt = jnp.dot(w_ukv_ref[...], kvl, preferred_element_type=F32)
    k_pe = _rope_cols(ht[OFF_KR: OFF_KR + MLA_ROPE_DIM], cos_b, sin_b)
    for h in range(MLA_HEADS):
        base = h * (MLA_NOPE_DIM + MLA_V_DIM)
        k_nope = kvt[base: base + MLA_NOPE_DIM]
        v = kvt[base + MLA_NOPE_DIM: base + MLA_NOPE_DIM + MLA_V_DIM]
        v = jnp.concatenate([v, ones_rows], axis=0).astype(BF16)
        for c in range(n_chunks):
            vm_ref[0, h, c] = v[:, c * TK:(c + 1) * TK]
        kt = jnp.concatenate([k_nope, k_pe, zeros_pad], axis=0)
        km_ref[0, h] = kt.T.astype(BF16)


def _pre_attention(x, mod, prep, ropes):
    B, S, _ = x.shape
    tm = TM_PRE
    nt = S // tm
    cpt = tm // TK
    nck = S // TK
    const = lambda shape: pl.BlockSpec(shape, lambda b, i: (0,) * len(shape))
    rope_a = pl.BlockSpec((GQA_HEAD_DIM // 2, tm), lambda b, i: (0, i))
    rope_b = pl.BlockSpec((MLA_ROPE_DIM // 2, tm), lambda b, i: (0, i))
    out_shape = (
        jax.ShapeDtypeStruct((B, GQA_HEADS, QK_PAD, S), BF16),
        jax.ShapeDtypeStruct((B, 1, S, QK_PAD), BF16),
        jax.ShapeDtypeStruct((B, GQA_KV_HEADS, nck, HEAD_V_EXT, TK), BF16),
        jax.ShapeDtypeStruct((B, MLA_HEADS, QK_PAD, S), BF16),
        jax.ShapeDtypeStruct((B, MLA_HEADS, S, QK_PAD), BF16),
        jax.ShapeDtypeStruct((B, MLA_HEADS, nck, HEAD_V_EXT, TK), BF16),
    )
    out_specs = (
        pl.BlockSpec((1, GQA_HEADS, QK_PAD, tm), lambda b, i: (b, 0, 0, i)),
        pl.BlockSpec((1, 1, tm, QK_PAD), lambda b, i: (b, 0, i, 0)),
        pl.BlockSpec((1, GQA_KV_HEADS, cpt, HEAD_V_EXT, TK), lambda b, i: (b, 0, i, 0, 0)),
        pl.BlockSpec((1, MLA_HEADS, QK_PAD, tm), lambda b, i: (b, 0, 0, i)),
        pl.BlockSpec((1, MLA_HEADS, tm, QK_PAD), lambda b, i: (b, 0, i, 0)),
        pl.BlockSpec((1, MLA_HEADS, cpt, HEAD_V_EXT, TK), lambda b, i: (b, 0, i, 0, 0)),
    )
    return pl.pallas_call(
        functools.partial(_pre_kernel, qscale_m=MLA_QK_DIM ** -0.5 * LOG2E),
        grid=(B, nt),
        in_specs=[
            pl.BlockSpec((1, tm, D_MODEL), lambda b, i: (b, i, 0)),
            pl.BlockSpec((1, 6, D_MODEL), lambda b, i: (b, 0, 0)),
            const((IN_WIDTH, D_MODEL)),
            const((MLA_HEADS * MLA_QK_DIM, MLA_Q_RANK)),
            const((MLA_HEADS * (MLA_NOPE_DIM + MLA_V_DIM), MLA_KV_RANK)),
            const((GQA_HEAD_DIM, 1)),
            const((GQA_HEAD_DIM, 1)),
            const((MLA_Q_RANK, 1)),
            const((MLA_KV_RANK, 1)),
            rope_a, rope_a, rope_b, rope_b,
        ],
        out_specs=out_specs,
        out_shape=out_shape,
        compiler_params=pltpu.CompilerParams(
            dimension_semantics=("parallel", "parallel"),
            vmem_limit_bytes=VMEM_LIMIT),
        name="pre_attention",
    )(x, mod, prep["w_in_t"], prep["w_uq_t"], prep["w_ukv_t"],
      prep["gq_col"], prep["gk_col"], prep["gql_col"], prep["gkvl_col"],
      ropes["cos_a"], ropes["sin_a"], ropes["cos_b"], ropes["sin_b"])


def _attn_kernel(q_ref, k_ref, v_ref, o_ref, s0_ref, s1_ref, cmax0_ref, cmax1_ref, p_ref, acc_ref,
                 *, grouped, n_chunks):
    def q_stream(i):
        if grouped:
            return q_ref[0, 0, i]
        return q_ref[0, 0, 0, :, i * STREAM_W:(i + 1) * STREAM_W]

    def load_k(c):
        off = pl.multiple_of(c * TK, TK)
        return k_ref[0, 0, pl.ds(off, TK), :]

    def qk(kc, slot, i):
        s_ref, cmax_ref = slot
        s = jnp.dot(kc, q_stream(i), preferred_element_type=F32)
        s_ref[i] = s
        cmax_ref[i] = jnp.max(s, axis=0, keepdims=True)

    def scores(c, slot):
        kc = load_k(c)
        for i in range(N_STREAMS):
            qk(kc, slot, i)

    def softmax(slot, i, m_old):
        s_ref, cmax_ref = slot
        m_new = jnp.maximum(m_old, cmax_ref[i])
        p_ref[i % 2] = jnp.exp2(s_ref[i] - m_new).astype(BF16)
        return m_new, jnp.exp2(m_old - m_new)

    def pv(c, i, alpha):
        acc_ref[i] = alpha * acc_ref[i] + jnp.dot(v_ref[0, 0, c], p_ref[i % 2],
                                                  preferred_element_type=F32)

    slots = ((s0_ref, cmax0_ref), (s1_ref, cmax1_ref))

    def trip(c0, ms, prefetch_next_trip):
        ms = list(ms)
        pending = None
        for u in range(CHUNKS_PER_TRIP):
            c = c0 + u
            produce = u + 1 < CHUNKS_PER_TRIP or prefetch_next_trip
            if produce:
                kc = load_k(c + 1)
            for i in range(N_STREAMS):
                if produce:
                    qk(kc, slots[(u + 1) % 2], i)
                ms[i], alpha = softmax(slots[u % 2], i, ms[i])
                if pending is not None:
                    pv(*pending)
                pending = (c, i, alpha)
        pv(*pending)
        return tuple(ms)

    acc_ref[...] = jnp.zeros(acc_ref.shape, F32)
    ms = tuple(jnp.full((1, STREAM_W), -jnp.inf, F32) for _ in range(N_STREAMS))
    scores(0, slots[0])
    ms = lax.fori_loop(0, n_chunks // CHUNKS_PER_TRIP - 1,
                       lambda t, ms: trip(t * CHUNKS_PER_TRIP, ms, True), ms)
    trip(n_chunks - CHUNKS_PER_TRIP, ms, False)
    for i in range(N_STREAMS):
        acc = acc_ref[i]
        o = acc[:HEAD_V] * (1.0 / acc[HEAD_V:HEAD_V + 1])
        if grouped:
            o_ref[0, 0, i] = o
        else:
            o_ref[0, 0, 0, :, i * STREAM_W:(i + 1) * STREAM_W] = o


def _attention(q_t, k, v_t, grouped):
    B, heads, _, S = q_t.shape
    n_chunks = S // TK
    assert n_chunks % CHUNKS_PER_TRIP == 0
    if grouped:
        kv = heads // N_STREAMS
        q5 = q_t.reshape(B, kv, N_STREAMS, QK_PAD, S)
        q_spec = pl.BlockSpec((1, 1, N_STREAMS, QK_PAD, STREAM_W), lambda b, j, i: (b, j, 0, 0, i))
        k_map = lambda b, j, i: (b, 0, 0, 0)
        o_shape = (B, kv, N_STREAMS, HEAD_V, S)
        o_spec = pl.BlockSpec((1, 1, N_STREAMS, HEAD_V, STREAM_W), lambda b, j, i: (b, j, 0, 0, i))
        nq = S // STREAM_W
    else:
        kv = heads
        tq = N_STREAMS * STREAM_W
        q5 = q_t.reshape(B, kv, 1, QK_PAD, S)
        q_spec = pl.BlockSpec((1, 1, 1, QK_PAD, tq), lambda b, j, i: (b, j, 0, 0, i))
        k_map = lambda b, j, i: (b, j, 0, 0)
        o_shape = (B, kv, 1, HEAD_V, S)
        o_spec = pl.BlockSpec((1, 1, 1, HEAD_V, tq), lambda b, j, i: (b, j, 0, 0, i))
        nq = S // tq
    out = pl.pallas_call(
        functools.partial(_attn_kernel, grouped=grouped, n_chunks=n_chunks),
        grid=(B, kv, nq),
        in_specs=[
            q_spec,
            pl.BlockSpec((1, 1, S, QK_PAD), k_map),
            pl.BlockSpec((1, 1, n_chunks, HEAD_V_EXT, TK), lambda b, j, i: (b, j, 0, 0, 0)),
        ],
        out_specs=o_spec,
        out_shape=jax.ShapeDtypeStruct(o_shape, F32),
        scratch_shapes=[
            pltpu.VMEM((N_STREAMS, TK, STREAM_W), F32),
            pltpu.VMEM((N_STREAMS, TK, STREAM_W), F32),
            pltpu.VMEM((N_STREAMS, 1, STREAM_W), F32),
            pltpu.VMEM((N_STREAMS, 1, STREAM_W), F32),
            pltpu.VMEM((2, TK, STREAM_W), BF16),
            pltpu.VMEM((N_STREAMS, HEAD_V_EXT, STREAM_W), F32),
        ],
        compiler_params=pltpu.CompilerParams(
            dimension_semantics=("parallel", "parallel", "parallel"),
            vmem_limit_bytes=VMEM_LIMIT),
        name="attention_grouped" if grouped else "attention_single",
    )(q5, k, v_t)
    return out.reshape(B, heads * HEAD_V, S)


def _post_kernel(x_ref, oa_ref, om_ref, mod_ref, ga_ref, gb_ref, w_out_ref,
                 ln1g_ref, ln1b_ref, w_gate_ref, w_up_ref, w_down_ref,
                 ln2g_ref, ln2b_ref, y_ref):
    x = x_ref[0]
    mod = mod_ref[0]
    g1, sh2, sc2, g2 = mod[2:3], mod[3:4], mod[4:5], mod[5:6]
    na = _rms_cols(oa_ref[0], ga_ref[...])
    nb = _rms_cols(om_ref[0], gb_ref[...])
    cat_t = jnp.concatenate([na, nb], axis=0).astype(BF16)
    mix = lax.dot_general(cat_t, w_out_ref[...], (((0,), (0,)), ((), ())),
                          preferred_element_type=F32)
    x1 = _layer_norm_rows(ALPHA * x + g1 * mix) * ln1g_ref[...] + ln1b_ref[...]
    u2 = (_layer_norm_rows(x1) * (1.0 + sc2) + sh2).astype(BF16)
    f = None
    for lo, hi in zip(FF_BOUNDS[:-1], FF_BOUNDS[1:]):
        cols = slice(lo, hi)
        gate = jnp.dot(u2, w_gate_ref[:, cols], preferred_element_type=F32)
        up = jnp.dot(u2, w_up_ref[:, cols], preferred_element_type=F32)
        act = (gate * (1.0 / (1.0 + jnp.exp(-gate))) * up).astype(BF16)
        part = jnp.dot(act, w_down_ref[cols, :], preferred_element_type=F32)
        f = part if f is None else f + part
    y_ref[0] = _layer_norm_rows(ALPHA * x1 + g2 * f) * ln2g_ref[...] + ln2b_ref[...]


def _post_attention(x, o_a, o_m, mod, prep):
    B, S, _ = x.shape
    tm = TM_POST
    single = pl.Buffered(1)
    const = lambda shape: pl.BlockSpec(shape, lambda b, i: (0,) * len(shape),
                                       pipeline_mode=single)
    return pl.pallas_call(
        _post_kernel,
        grid=(B, S // tm),
        in_specs=[
            pl.BlockSpec((1, tm, D_MODEL), lambda b, i: (b, i, 0)),
            pl.BlockSpec((1, GQA_WIDTH, tm), lambda b, i: (b, 0, i)),
            pl.BlockSpec((1, MLA_WIDTH, tm), lambda b, i: (b, 0, i)),
            pl.BlockSpec((1, 6, D_MODEL), lambda b, i: (b, 0, 0)),
            const((GQA_WIDTH, 1)),
            const((MLA_WIDTH, 1)),
            const((MIX_WIDTH, D_MODEL)),
            const((1, D_MODEL)),
            const((1, D_MODEL)),
            const((D_MODEL, D_FF)),
            const((D_MODEL, D_FF)),
            const((D_FF, D_MODEL)),
            const((1, D_MODEL)),
            const((1, D_MODEL)),
        ],
        out_specs=pl.BlockSpec((1, tm, D_MODEL), lambda b, i: (b, i, 0)),
        out_shape=jax.ShapeDtypeStruct((B, S, D_MODEL), F32),
        compiler_params=pltpu.CompilerParams(
            dimension_semantics=("parallel", "parallel"),
            vmem_limit_bytes=VMEM_LIMIT),
        name="post_attention",
    )(x, o_a, o_m, mod, prep["ga_col"], prep["gb_col"], prep["w_out"],
      prep["ln1_g"], prep["ln1_b"], prep["w_gate"], prep["w_up"], prep["w_down"],
      prep["ln2_g"], prep["ln2_b"])


def _rope_tables(seq_len, rot_dim):
    rows = seq_len // GRID_W
    row = jnp.repeat(jnp.arange(rows, dtype=F32), GRID_W)
    col = jnp.tile(jnp.arange(GRID_W, dtype=F32), rows)
    n = rot_dim // 4
    inv_freq = 1.0 / jnp.power(ROPE_THETA, jnp.arange(n, dtype=F32) / n)
    ang = jnp.concatenate([inv_freq[:, None] * row[None, :],
                           inv_freq[:, None] * col[None, :]], axis=0)
    return jnp.cos(ang), jnp.sin(ang)


def _trunk(x, mod, prep):
    S = x.shape[1]
    cos_a, sin_a = _rope_tables(S, GQA_HEAD_DIM)
    cos_b, sin_b = _rope_tables(S, MLA_ROPE_DIM)
    ropes = dict(cos_a=cos_a, sin_a=sin_a, cos_b=cos_b, sin_b=sin_b)
    qa_t, k_a, va_t, qm_t, k_m, vm_t = _pre_attention(x, mod, prep, ropes)
    o_a = _attention(qa_t, k_a, va_t, True)
    o_m = _attention(qm_t, k_m, vm_t, False)
    return _post_attention(x, o_a, o_m, mod, prep)


def kernel(x_prompt, x_sample, c_prompt, c_sample, w_ada, b_ada, w_in, gqa_q_norm, gqa_k_norm,
           mla_q_norm, w_uq, mla_kv_norm, w_ukv, grp_norm_a, grp_norm_b, w_out,
           ln1_g, ln1_b, w_gate, w_up, w_down, ln2_g, ln2_b):
    l = 0
    col = lambda v: v.reshape(-1, 1).astype(F32)
    row = lambda v: v.reshape(1, -1).astype(F32)
    prep = dict(
        w_in_t=w_in[l].T.astype(BF16),
        w_uq_t=w_uq[l].T.astype(BF16),
        w_ukv_t=w_ukv[l].T.astype(BF16),
        gq_col=col(gqa_q_norm[l] * (GQA_HEAD_DIM ** -0.5 * LOG2E)),
        gk_col=col(gqa_k_norm[l]),
        gql_col=col(mla_q_norm[l]),
        gkvl_col=col(mla_kv_norm[l]),
        ga_col=col(grp_norm_a[l]),
        gb_col=col(grp_norm_b[l]),
        w_out=w_out[l].astype(BF16),
        ln1_g=row(ln1_g[l]), ln1_b=row(ln1_b[l]),
        w_gate=w_gate[l].astype(BF16),
        w_up=w_up[l].astype(BF16),
        w_down=w_down[l].astype(BF16),
        ln2_g=row(ln2_g[l]), ln2_b=row(ln2_b[l]),
    )
    nb = c_prompt.shape[0]
    mod = _ada_mod(jnp.concatenate([c_prompt, c_sample], axis=0), w_ada[l], b_ada[l])
    mod = mod.reshape(-1, 6, D_MODEL)
    y_prompt = _trunk(x_prompt, mod[:nb], prep)
    y_sample = _trunk(x_sample, mod[nb:], prep)
    return (y_prompt, y_sample)
```

```python
import functools
import math

import jax
import jax.numpy as jnp
from jax import lax
from jax.experimental import pallas as pl
from jax.experimental.pallas import tpu as pltpu

D_MODEL = 1024
DEPTH = 1
GRID_W = 64
ROPE_THETA = 10000.0
NORM_EPS = 1e-6
GQA_HEADS = 8
GQA_KV_HEADS = 2
GQA_GROUP = GQA_HEADS // GQA_KV_HEADS
GQA_HEAD_DIM = 64
MLA_HEADS = 8
MLA_Q_RANK = 256
MLA_KV_RANK = 128
MLA_NOPE_DIM = 64
MLA_ROPE_DIM = 32
MLA_QK_DIM = MLA_NOPE_DIM + MLA_ROPE_DIM
MLA_V_DIM = 64
GQA_WIDTH = GQA_HEADS * GQA_HEAD_DIM
MLA_WIDTH = MLA_HEADS * MLA_V_DIM
MIX_WIDTH = GQA_WIDTH + MLA_WIDTH
GQA_KV_WIDTH = GQA_KV_HEADS * GQA_HEAD_DIM
IN_WIDTH = GQA_WIDTH + 2 * GQA_KV_WIDTH + MLA_Q_RANK + MLA_KV_RANK + MLA_ROPE_DIM
D_FF = -(-8 * D_MODEL // (3 * 256)) * 256
ALPHA = (2.0 * DEPTH) ** 0.25

OFF_QA = 0
OFF_KA = OFF_QA + GQA_WIDTH
OFF_VA = OFF_KA + GQA_KV_WIDTH
OFF_QL = OFF_VA + GQA_KV_WIDTH
OFF_KVL = OFF_QL + MLA_Q_RANK
OFF_KR = OFF_KVL + MLA_KV_RANK

QK_PAD = 128
HEAD_V = 64
ONES_ROWS = 16
HEAD_V_EXT = HEAD_V + ONES_ROWS
LOG2E = math.log2(math.e)

TM_PRE = 512
TK = 512
CHUNKS_PER_TRIP = 4
N_STREAMS = GQA_GROUP
STREAM_W = 256
TM_POST = 512
MXU_TILE = 256
FF_BOUNDS = (0, (D_FF // MXU_TILE + 1) // 2 * MXU_TILE, D_FF)
VMEM_LIMIT = 56 * 1024 * 1024

F32 = jnp.float32
BF16 = jnp.bfloat16


def _layer_norm_rows(x):
    mu = jnp.mean(x, axis=-1, keepdims=True)
    xc = x - mu
    var = jnp.mean(xc * xc, axis=-1, keepdims=True)
    return xc * lax.rsqrt(var + NORM_EPS)


def _rms_cols(xt, gain_col):
    ms = jnp.mean(xt * xt, axis=0, keepdims=True)
    return xt * lax.rsqrt(ms + NORM_EPS) * gain_col


def _rope_cols(xt, cos_t, sin_t):
    half = xt.shape[0] // 2
    x1, x2 = xt[:half], xt[half:]
    return jnp.concatenate([x1 * cos_t - x2 * sin_t, x1 * sin_t + x2 * cos_t], axis=0)


def _ada_kernel(c_ref, w_ref, b_ref, o_ref):
    c = c_ref[...]
    s = c * (1.0 / (1.0 + jnp.exp(-c)))
    o_ref[...] = jnp.dot(s, w_ref[...], preferred_element_type=F32) + b_ref[...]


def _ada_mod(c, w_ada, b_ada):
    n = c.shape[0]
    cols = 6 * D_MODEL
    return pl.pallas_call(
        _ada_kernel,
        grid=(6,),
        in_specs=[
            pl.BlockSpec((n, D_MODEL), lambda j: (0, 0)),
            pl.BlockSpec((D_MODEL, D_MODEL), lambda j: (0, j)),
            pl.BlockSpec((1, D_MODEL), lambda j: (0, j)),
        ],
        out_specs=pl.BlockSpec((n, D_MODEL), lambda j: (0, j)),
        out_shape=jax.ShapeDtypeStruct((n, cols), F32),
        name="ada_mod",
    )(c, w_ada, b_ada.reshape(1, cols))


def _pre_kernel(x_ref, mod_ref, w_in_ref, w_uq_ref, w_ukv_ref,
                gq_ref, gk_ref, gql_ref, gkvl_ref,
                cosa_ref, sina_ref, cosb_ref, sinb_ref,
                qa_ref, ka_ref, va_ref, qm_ref, km_ref, vm_ref, *, qscale_m):
    x = x_ref[0]
    mod = mod_ref[0]
    sh1, sc1 = mod[0:1], mod[1:2]
    u = (_layer_norm_rows(x) * (1.0 + sc1) + sh1).astype(BF16)
    ht = lax.dot_general(w_in_ref[...], u, (((1,), (1,)), ((), ())),
                         preferred_element_type=F32)
    tm = ht.shape[1]
    n_chunks = tm // TK
    cos_a, sin_a = cosa_ref[...], sina_ref[...]
    cos_b, sin_b = cosb_ref[...], sinb_ref[...]
    zeros_half = jnp.zeros((GQA_HEAD_DIM, tm), F32)
    ones_rows = jnp.where(lax.broadcasted_iota(jnp.int32, (ONES_ROWS, tm), 0) == 0, 1.0, 0.0)

    gq = gq_ref[...]
    for h in range(GQA_HEADS):
        slab = ht[OFF_QA + h * GQA_HEAD_DIM: OFF_QA + (h + 1) * GQA_HEAD_DIM]
        q = _rope_cols(_rms_cols(slab, gq), cos_a, sin_a)
        parts = [zeros_half] * GQA_KV_HEADS
        parts[h // GQA_GROUP] = q
        qa_ref[0, h] = jnp.concatenate(parts, axis=0).astype(BF16)

    gk = gk_ref[...]
    k_parts = []
    for j in range(GQA_KV_HEADS):
        slab = ht[OFF_KA + j * GQA_HEAD_DIM: OFF_KA + (j + 1) * GQA_HEAD_DIM]
        k_parts.append(_rope_cols(_rms_cols(slab, gk), cos_a, sin_a))
        v = ht[OFF_VA + j * GQA_HEAD_DIM: OFF_VA + (j + 1) * GQA_HEAD_DIM]
        v = jnp.concatenate([v, ones_rows], axis=0).astype(BF16)
        for c in range(n_chunks):
            va_ref[0, j, c] = v[:, c * TK:(c + 1) * TK]
    ka_ref[0, 0] = jnp.concatenate(k_parts, axis=0).T.astype(BF16)

    ql = _rms_cols(ht[OFF_QL: OFF_QL + MLA_Q_RANK], gql_ref[...]).astype(BF16)
    qmt = jnp.dot(w_uq_ref[...], ql, preferred_element_type=F32)
    zeros_pad = jnp.zeros((QK_PAD - MLA_QK_DIM, tm), F32)
    for h in range(MLA_HEADS):
        base = h * MLA_QK_DIM
        nope = qmt[base: base + MLA_NOPE_DIM]
        pe = _rope_cols(qmt[base + MLA_NOPE_DIM: base + MLA_QK_DIM], cos_b, sin_b)
        q = jnp.concatenate([nope, pe], axis=0) * qscale_m
        qm_ref[0, h] = jnp.concatenate([q, zeros_pad], axis=0).astype(BF16)

    kvl = _rms_cols(ht[OFF_KVL: OFF_KVL + MLA_KV_RANK], gkvl_ref[...]).astype(BF16)
    kvt = jnp.dot(w_ukv_ref[...], kvl, preferred_element_type=F32)
    k_pe = _rope_cols(ht[OFF_KR: OFF_KR + MLA_ROPE_DIM], cos_b, sin_b)
    for h in range(MLA_HEADS):
        base = h * (MLA_NOPE_DIM + MLA_V_DIM)
        k_nope = kvt[base: base + MLA_NOPE_DIM]
        v = kvt[base + MLA_NOPE_DIM: base + MLA_NOPE_DIM + MLA_V_DIM]
        v = jnp.concatenate([v, ones_rows], axis=0).astype(BF16)
        for c in range(n_chunks):
            vm_ref[0, h, c] = v[:, c * TK:(c + 1) * TK]
        kt = jnp.concatenate([k_nope, k_pe, zeros_pad], axis=0)
        km_ref[0, h] = kt.T.astype(BF16)


def _pre_attention(x, mod, prep, ropes):
    B, S, _ = x.shape
    tm = TM_PRE
    nt = S // tm
    cpt = tm // TK
    nck = S // TK
    const = lambda shape: pl.BlockSpec(shape, lambda b, i: (0,) * len(shape))
    rope_a = pl.BlockSpec((GQA_HEAD_DIM // 2, tm), lambda b, i: (0, i))
    rope_b = pl.BlockSpec((MLA_ROPE_DIM // 2, tm), lambda b, i: (0, i))
    out_shape = (
        jax.ShapeDtypeStruct((B, GQA_HEADS, QK_PAD, S), BF16),
        jax.ShapeDtypeStruct((B, 1, S, QK_PAD), BF16),
        jax.ShapeDtypeStruct((B, GQA_KV_HEADS, nck, HEAD_V_EXT, TK), BF16),
        jax.ShapeDtypeStruct((B, MLA_HEADS, QK_PAD, S), BF16),
        jax.ShapeDtypeStruct((B, MLA_HEADS, S, QK_PAD), BF16),
        jax.ShapeDtypeStruct((B, MLA_HEADS, nck, HEAD_V_EXT, TK), BF16),
    )
    out_specs = (
        pl.BlockSpec((1, GQA_HEADS, QK_PAD, tm), lambda b, i: (b, 0, 0, i)),
        pl.BlockSpec((1, 1, tm, QK_PAD), lambda b, i: (b, 0, i, 0)),
        pl.BlockSpec((1, GQA_KV_HEADS, cpt, HEAD_V_EXT, TK), lambda b, i: (b, 0, i, 0, 0)),
        pl.BlockSpec((1, MLA_HEADS, QK_PAD, tm), lambda b, i: (b, 0, 0, i)),
        pl.BlockSpec((1, MLA_HEADS, tm, QK_PAD), lambda b, i: (b, 0, i, 0)),
        pl.BlockSpec((1, MLA_HEADS, cpt, HEAD_V_EXT, TK), lambda b, i: (b, 0, i, 0, 0)),
    )
    return pl.pallas_call(
        functools.partial(_pre_kernel, qscale_m=MLA_QK_DIM ** -0.5 * LOG2E),
        grid=(B, nt),
        in_specs=[
            pl.BlockSpec((1, tm, D_MODEL), lambda b, i: (b, i, 0)),
            pl.BlockSpec((1, 6, D_MODEL), lambda b, i: (b, 0, 0)),
            const((IN_WIDTH, D_MODEL)),
            const((MLA_HEADS * MLA_QK_DIM, MLA_Q_RANK)),
            const((MLA_HEADS * (MLA_NOPE_DIM + MLA_V_DIM), MLA_KV_RANK)),
            const((GQA_HEAD_DIM, 1)),
            const((GQA_HEAD_DIM, 1)),
            const((MLA_Q_RANK, 1)),
            const((MLA_KV_RANK, 1)),
            rope_a, rope_a, rope_b, rope_b,
        ],
        out_specs=out_specs,
        out_shape=out_shape,
        compiler_params=pltpu.CompilerParams(
            dimension_semantics=("parallel", "parallel"),
            vmem_limit_bytes=VMEM_LIMIT),
        name="pre_attention",
    )(x, mod, prep["w_in_t"], prep["w_uq_t"], prep["w_ukv_t"],
      prep["gq_col"], prep["gk_col"], prep["gql_col"], prep["gkvl_col"],
      ropes["cos_a"], ropes["sin_a"], ropes["cos_b"], ropes["sin_b"])


def _attn_kernel(q_ref, k_ref, v_ref, o_ref, s0_ref, s1_ref, cmax0_ref, cmax1_ref, acc_ref,
                 *, grouped, n_chunks):
    def q_stream(i):
        if grouped:
            return q_ref[0, 0, i]
        return q_ref[0, 0, 0, :, i * STREAM_W:(i + 1) * STREAM_W]

    def load_k(c):
        off = pl.multiple_of(c * TK, TK)
        return k_ref[0, 0, pl.ds(off, TK), :]

    def qk(kc, slot, i):
        s_ref, cmax_ref = slot
        s = jnp.dot(kc, q_stream(i), preferred_element_type=F32)
        s_ref[i] = s
        cmax_ref[i] = jnp.max(s, axis=0, keepdims=True)

    def scores(c, slot):
        kc = load_k(c)
        for i in range(N_STREAMS):
            qk(kc, slot, i)

    def softmax(slot, i, m_old):
        s_ref, cmax_ref = slot
        m_new = jnp.maximum(m_old, cmax_ref[i])
        p = jnp.exp2(s_ref[i] - m_new).astype(BF16)
        return m_new, jnp.exp2(m_old - m_new), p

    def pv(c, i, alpha, p):
        acc_ref[i] = alpha * acc_ref[i] + jnp.dot(v_ref[0, 0, c], p, preferred_element_type=F32)

    slots = ((s0_ref, cmax0_ref), (s1_ref, cmax1_ref))

    def trip(c0, ms, prefetch_next_trip):
        ms = list(ms)
        for u in range(CHUNKS_PER_TRIP):
            c = c0 + u
            produce = u + 1 < CHUNKS_PER_TRIP or prefetch_next_trip
            if produce:
                kc = load_k(c + 1)
            for i in range(N_STREAMS):
                if produce:
                    qk(kc, slots[(u + 1) % 2], i)
                ms[i], alpha, p = softmax(slots[u % 2], i, ms[i])
                pv(c, i, alpha, p)
        return tuple(ms)

    acc_ref[...] = jnp.zeros(acc_ref.shape, F32)
    ms = tuple(jnp.full((1, STREAM_W), -jnp.inf, F32) for _ in range(N_STREAMS))
    scores(0, slots[0])
    ms = lax.fori_loop(0, n_chunks // CHUNKS_PER_TRIP - 1,
                       lambda t, ms: trip(t * CHUNKS_PER_TRIP, ms, True), ms)
    trip(n_chunks - CHUNKS_PER_TRIP, ms, False)
    for i in range(N_STREAMS):
        acc = acc_ref[i]
        o = acc[:HEAD_V] * (1.0 / acc[HEAD_V:HEAD_V + 1])
        if grouped:
            o_ref[0, 0, i] = o
        else:
            o_ref[0, 0, 0, :, i * STREAM_W:(i + 1) * STREAM_W] = o


def _attention(q_t, k, v_t, grouped):
    B, heads, _, S = q_t.shape
    n_chunks = S // TK
    assert n_chunks % CHUNKS_PER_TRIP == 0
    if grouped:
        kv = heads // N_STREAMS
        q5 = q_t.reshape(B, kv, N_STREAMS, QK_PAD, S)
        q_spec = pl.BlockSpec((1, 1, N_STREAMS, QK_PAD, STREAM_W), lambda b, j, i: (b, j, 0, 0, i))
        k_map = lambda b, j, i: (b, 0, 0, 0)
        o_shape = (B, kv, N_STREAMS, HEAD_V, S)
        o_spec = pl.BlockSpec((1, 1, N_STREAMS, HEAD_V, STREAM_W), lambda b, j, i: (b, j, 0, 0, i))
        nq = S // STREAM_W
    else:
        kv = heads
        tq = N_STREAMS * STREAM_W
        q5 = q_t.reshape(B, kv, 1, QK_PAD, S)
        q_spec = pl.BlockSpec((1, 1, 1, QK_PAD, tq), lambda b, j, i: (b, j, 0, 0, i))
        k_map = lambda b, j, i: (b, j, 0, 0)
        o_shape = (B, kv, 1, HEAD_V, S)
        o_spec = pl.BlockSpec((1, 1, 1, HEAD_V, tq), lambda b, j, i: (b, j, 0, 0, i))
        nq = S // tq
    out = pl.pallas_call(
        functools.partial(_attn_kernel, grouped=grouped, n_chunks=n_chunks),
        grid=(B, kv, nq),
        in_specs=[
            q_spec,
            pl.BlockSpec((1, 1, S, QK_PAD), k_map),
            pl.BlockSpec((1, 1, n_chunks, HEAD_V_EXT, TK), lambda b, j, i: (b, j, 0, 0, 0)),
        ],
        out_specs=o_spec,
        out_shape=jax.ShapeDtypeStruct(o_shape, F32),
        scratch_shapes=[
            pltpu.VMEM((N_STREAMS, TK, STREAM_W), F32),
            pltpu.VMEM((N_STREAMS, TK, STREAM_W), F32),
            pltpu.VMEM((N_STREAMS, 1, STREAM_W), F32),
            pltpu.VMEM((N_STREAMS, 1, STREAM_W), F32),
            pltpu.VMEM((N_STREAMS, HEAD_V_EXT, STREAM_W), F32),
        ],
        compiler_params=pltpu.CompilerParams(
            dimension_semantics=("parallel", "parallel", "parallel"),
            vmem_limit_bytes=VMEM_LIMIT),
        name="attention_grouped" if grouped else "attention_single",
    )(q5, k, v_t)
    return out.reshape(B, heads * HEAD_V, S)


def _post_kernel(x_ref, oa_ref, om_ref, mod_ref, ga_ref, gb_ref, w_out_ref,
                 ln1g_ref, ln1b_ref, w_gate_ref, w_up_ref, w_down_ref,
                 ln2g_ref, ln2b_ref, y_ref):
    x = x_ref[0]
    mod = mod_ref[0]
    g1, sh2, sc2, g2 = mod[2:3], mod[3:4], mod[4:5], mod[5:6]
    na = _rms_cols(oa_ref[0], ga_ref[...])
    nb = _rms_cols(om_ref[0], gb_ref[...])
    cat_t = jnp.concatenate([na, nb], axis=0).astype(BF16)
    mix = lax.dot_general(cat_t, w_out_ref[...], (((0,), (0,)), ((), ())),
                          preferred_element_type=F32)
    x1 = _layer_norm_rows(ALPHA * x + g1 * mix) * ln1g_ref[...] + ln1b_ref[...]
    u2 = (_layer_norm_rows(x1) * (1.0 + sc2) + sh2).astype(BF16)
    f = None
    for lo, hi in zip(FF_BOUNDS[:-1], FF_BOUNDS[1:]):
        cols = slice(lo, hi)
        gate = jnp.dot(u2, w_gate_ref[:, cols], preferred_element_type=F32)
        up = jnp.dot(u2, w_up_ref[:, cols], preferred_element_type=F32)
        act = (gate * (1.0 / (1.0 + jnp.exp(-gate))) * up).astype(BF16)
        part = jnp.dot(act, w_down_ref[cols, :], preferred_element_type=F32)
        f = part if f is None else f + part
    y_ref[0] = _layer_norm_rows(ALPHA * x1 + g2 * f) * ln2g_ref[...] + ln2b_ref[...]


def _post_attention(x, o_a, o_m, mod, prep):
    B, S, _ = x.shape
    tm = TM_POST
    single = pl.Buffered(1)
    const = lambda shape: pl.BlockSpec(shape, lambda b, i: (0,) * len(shape),
                                       pipeline_mode=single)
    return pl.pallas_call(
        _post_kernel,
        grid=(B, S // tm),
        in_specs=[
            pl.BlockSpec((1, tm, D_MODEL), lambda b, i: (b, i, 0)),
            pl.BlockSpec((1, GQA_WIDTH, tm), lambda b, i: (b, 0, i)),
            pl.BlockSpec((1, MLA_WIDTH, tm), lambda b, i: (b, 0, i)),
            pl.BlockSpec((1, 6, D_MODEL), lambda b, i: (b, 0, 0)),
            const((GQA_WIDTH, 1)),
            const((MLA_WIDTH, 1)),
            const((MIX_WIDTH, D_MODEL)),
            const((1, D_MODEL)),
            const((1, D_MODEL)),
            const((D_MODEL, D_FF)),
            const((D_MODEL, D_FF)),
            const((D_FF, D_MODEL)),
            const((1, D_MODEL)),
            const((1, D_MODEL)),
        ],
        out_specs=pl.BlockSpec((1, tm, D_MODEL), lambda b, i: (b, i, 0)),
        out_shape=jax.ShapeDtypeStruct((B, S, D_MODEL), F32),
        compiler_params=pltpu.CompilerParams(
            dimension_semantics=("parallel", "parallel"),
            vmem_limit_bytes=VMEM_LIMIT),
        name="post_attention",
    )(x, o_a, o_m, mod, prep["ga_col"], prep["gb_col"], prep["w_out"],
      prep["ln1_g"], prep["ln1_b"], prep["w_gate"], prep["w_up"], prep["w_down"],
      prep["ln2_g"], prep["ln2_b"])


def _rope_tables(seq_len, rot_dim):
    rows = seq_len // GRID_W
    row = jnp.repeat(jnp.arange(rows, dtype=F32), GRID_W)
    col = jnp.tile(jnp.arange(GRID_W, dtype=F32), rows)
    n = rot_dim // 4
    inv_freq = 1.0 / jnp.power(ROPE_THETA, jnp.arange(n, dtype=F32) / n)
    ang = jnp.concatenate([inv_freq[:, None] * row[None, :],
                           inv_freq[:, None] * col[None, :]], axis=0)
    return jnp.cos(ang), jnp.sin(ang)


def _trunk(x, mod, prep):
    S = x.shape[1]
    cos_a, sin_a = _rope_tables(S, GQA_HEAD_DIM)
    cos_b, sin_b = _rope_tables(S, MLA_ROPE_DIM)
    ropes = dict(cos_a=cos_a, sin_a=sin_a, cos_b=cos_b, sin_b=sin_b)
    qa_t, k_a, va_t, qm_t, k_m, vm_t = _pre_attention(x, mod, prep, ropes)
    o_a = _attention(qa_t, k_a, va_t, True)
    o_m = _attention(qm_t, k_m, vm_t, False)
    return _post_attention(x, o_a, o_m, mod, prep)


def kernel(x_prompt, x_sample, c_prompt, c_sample, w_ada, b_ada, w_in, gqa_q_norm, gqa_k_norm,
           mla_q_norm, w_uq, mla_kv_norm, w_ukv, grp_norm_a, grp_norm_b, w_out,
           ln1_g, ln1_b, w_gate, w_up, w_down, ln2_g, ln2_b):
    l = 0
    col = lambda v: v.reshape(-1, 1).astype(F32)
    row = lambda v: v.reshape(1, -1).astype(F32)
    prep = dict(
        w_in_t=w_in[l].T.astype(BF16),
        w_uq_t=w_uq[l].T.astype(BF16),
        w_ukv_t=w_ukv[l].T.astype(BF16),
        gq_col=col(gqa_q_norm[l] * (GQA_HEAD_DIM ** -0.5 * LOG2E)),
        gk_col=col(gqa_k_norm[l]),
        gql_col=col(mla_q_norm[l]),
        gkvl_col=col(mla_kv_norm[l]),
        ga_col=col(grp_norm_a[l]),
        gb_col=col(grp_norm_b[l]),
        w_out=w_out[l].astype(BF16),
        ln1_g=row(ln1_g[l]), ln1_b=row(ln1_b[l]),
        w_gate=w_gate[l].astype(BF16),
        w_up=w_up[l].astype(BF16),
        w_down=w_down[l].astype(BF16),
        ln2_g=row(ln2_g[l]), ln2_b=row(ln2_b[l]),
    )
    nb = c_prompt.shape[0]
    mod = _ada_mod(jnp.concatenate([c_prompt, c_sample], axis=0), w_ada[l], b_ada[l])
    mod = mod.reshape(-1, 6, D_MODEL)
    y_prompt = _trunk(x_prompt, mod[:nb], prep)
    y_sample = _trunk(x_sample, mod[nb:], prep)
    return (y_prompt, y_sample)
```

```python
import functools
import math

import jax
import jax.numpy as jnp
from jax import lax
from jax.experimental import pallas as pl
from jax.experimental.pallas import tpu as pltpu

D_MODEL = 1024
DEPTH = 1
GRID_W = 64
ROPE_THETA = 10000.0
NORM_EPS = 1e-6
GQA_HEADS = 8
GQA_KV_HEADS = 2
GQA_GROUP = GQA_HEADS // GQA_KV_HEADS
GQA_HEAD_DIM = 64
MLA_HEADS = 8
MLA_Q_RANK = 256
MLA_KV_RANK = 128
MLA_NOPE_DIM = 64
MLA_ROPE_DIM = 32
MLA_QK_DIM = MLA_NOPE_DIM + MLA_ROPE_DIM
MLA_V_DIM = 64
GQA_WIDTH = GQA_HEADS * GQA_HEAD_DIM
MLA_WIDTH = MLA_HEADS * MLA_V_DIM
MIX_WIDTH = GQA_WIDTH + MLA_WIDTH
GQA_KV_WIDTH = GQA_KV_HEADS * GQA_HEAD_DIM
IN_WIDTH = GQA_WIDTH + 2 * GQA_KV_WIDTH + MLA_Q_RANK + MLA_KV_RANK + MLA_ROPE_DIM
D_FF = -(-8 * D_MODEL // (3 * 256)) * 256
ALPHA = (2.0 * DEPTH) ** 0.25

OFF_QA = 0
OFF_KA = OFF_QA + GQA_WIDTH
OFF_VA = OFF_KA + GQA_KV_WIDTH
OFF_QL = OFF_VA + GQA_KV_WIDTH
OFF_KVL = OFF_QL + MLA_Q_RANK
OFF_KR = OFF_KVL + MLA_KV_RANK

QK_PAD = 128
HEAD_V = 64
ONES_ROWS = 16
HEAD_V_EXT = HEAD_V + ONES_ROWS
LOG2E = math.log2(math.e)

TM_PRE = 512
TK = 512
CHUNKS_PER_TRIP = 4
N_STREAMS = 8
STREAM_W = 256
TM_POST = 512
MXU_TILE = 256
FF_BOUNDS = (0, (D_FF // MXU_TILE + 1) // 2 * MXU_TILE, D_FF)
VMEM_LIMIT = 56 * 1024 * 1024

F32 = jnp.float32
BF16 = jnp.bfloat16


def _layer_norm_rows(x):
    mu = jnp.mean(x, axis=-1, keepdims=True)
    xc = x - mu
    var = jnp.mean(xc * xc, axis=-1, keepdims=True)
    return xc * lax.rsqrt(var + NORM_EPS)


def _rms_cols(xt, gain_col):
    ms = jnp.mean(xt * xt, axis=0, keepdims=True)
    return xt * lax.rsqrt(ms + NORM_EPS) * gain_col


def _rope_cols(xt, cos_t, sin_t):
    half = xt.shape[0] // 2
    x1, x2 = xt[:half], xt[half:]
    return jnp.concatenate([x1 * cos_t - x2 * sin_t, x1 * sin_t + x2 * cos_t], axis=0)


def _ada_kernel(c_ref, w_ref, b_ref, o_ref):
    c = c_ref[...]
    s = c * (1.0 / (1.0 + jnp.exp(-c)))
    o_ref[...] = jnp.dot(s, w_ref[...], preferred_element_type=F32) + b_ref[...]


def _ada_mod(c, w_ada, b_ada):
    n = c.shape[0]
    cols = 6 * D_MODEL
    return pl.pallas_call(
        _ada_kernel,
        grid=(6,),
        in_specs=[
            pl.BlockSpec((n, D_MODEL), lambda j: (0, 0)),
            pl.BlockSpec((D_MODEL, D_MODEL), lambda j: (0, j)),
            pl.BlockSpec((1, D_MODEL), lambda j: (0, j)),
        ],
        out_specs=pl.BlockSpec((n, D_MODEL), lambda j: (0, j)),
        out_shape=jax.ShapeDtypeStruct((n, cols), F32),
        name="ada_mod",
    )(c, w_ada, b_ada.reshape(1, cols))


def _pre_kernel(x_ref, mod_ref, w_in_ref, w_uq_ref, w_ukv_ref,
                gq_ref, gk_ref, gql_ref, gkvl_ref,
                cosa_ref, sina_ref, cosb_ref, sinb_ref,
                qa_ref, ka_ref, va_ref, qm_ref, km_ref, vm_ref, *, qscale_m):
    x = x_ref[0]
    mod = mod_ref[0]
    sh1, sc1 = mod[0:1], mod[1:2]
    u = (_layer_norm_rows(x) * (1.0 + sc1) + sh1).astype(BF16)
    ht = lax.dot_general(w_in_ref[...], u, (((1,), (1,)), ((), ())),
                         preferred_element_type=F32)
    tm = ht.shape[1]
    n_chunks = tm // TK
    cos_a, sin_a = cosa_ref[...], sina_ref[...]
    cos_b, sin_b = cosb_ref[...], sinb_ref[...]
    zeros_half = jnp.zeros((GQA_HEAD_DIM, tm), F32)
    ones_rows = jnp.where(lax.broadcasted_iota(jnp.int32, (ONES_ROWS, tm), 0) == 0, 1.0, 0.0)

    gq = gq_ref[...]
    for h in range(GQA_HEADS):
        slab = ht[OFF_QA + h * GQA_HEAD_DIM: OFF_QA + (h + 1) * GQA_HEAD_DIM]
        q = _rope_cols(_rms_cols(slab, gq), cos_a, sin_a)
        parts = [zeros_half] * GQA_KV_HEADS
        parts[h // GQA_GROUP] = q
        qa_ref[0, h] = jnp.concatenate(parts, axis=0).astype(BF16)

    gk = gk_ref[...]
    k_parts = []
    for j in range(GQA_KV_HEADS):
        slab = ht[OFF_KA + j * GQA_HEAD_DIM: OFF_KA + (j + 1) * GQA_HEAD_DIM]
        k_parts.append(_rope_cols(_rms_cols(slab, gk), cos_a, sin_a))
        v = ht[OFF_VA + j * GQA_HEAD_DIM: OFF_VA + (j + 1) * GQA_HEAD_DIM]
        v = jnp.concatenate([v, ones_rows], axis=0).astype(BF16)
        for c in range(n_chunks):
            va_ref[0, j, c] = v[:, c * TK:(c + 1) * TK]
    ka_ref[0, 0] = jnp.concatenate(k_parts, axis=0).T.astype(BF16)

    ql = _rms_cols(ht[OFF_QL: OFF_QL + MLA_Q_RANK], gql_ref[...]).astype(BF16)
    qmt = jnp.dot(w_uq_ref[...], ql, preferred_element_type=F32)
    zeros_pad = jnp.zeros((QK_PAD - MLA_QK_DIM, tm), F32)
    for h in range(MLA_HEADS):
        base = h * MLA_QK_DIM
        nope = qmt[base: base + MLA_NOPE_DIM]
        pe = _rope_cols(qmt[base + MLA_NOPE_DIM: base + MLA_QK_DIM], cos_b, sin_b)
        q = jnp.concatenate([nope, pe], axis=0) * qscale_m
        qm_ref[0, h] = jnp.concatenate([q, zeros_pad], axis=0).astype(BF16)

    kvl = _rms_cols(ht[OFF_KVL: OFF_KVL + MLA_KV_RANK], gkvl_ref[...]).astype(BF16)
    kvt = jnp.dot(w_ukv_ref[...], kvl, preferred_element_type=F32)
    k_pe = _rope_cols(ht[OFF_KR: OFF_KR + MLA_ROPE_DIM], cos_b, sin_b)
    for h in range(MLA_HEADS):
        base = h * (MLA_NOPE_DIM + MLA_V_DIM)
        k_nope = kvt[base: base + MLA_NOPE_DIM]
        v = kvt[base + MLA_NOPE_DIM: base + MLA_NOPE_DIM + MLA_V_DIM]
        v = jnp.concatenate([v, ones_rows], axis=0).astype(BF16)
        for c in range(n_chunks):
            vm_ref[0, h, c] = v[:, c * TK:(c + 1) * TK]
        kt = jnp.concatenate([k_nope, k_pe, zeros_pad], axis=0)
        km_ref[0, h] = kt.T.astype(BF16)


def _pre_attention(x, mod, prep, ropes):
    B, S, _ = x.shape
    tm = TM_PRE
    nt = S // tm
    cpt = tm // TK
    nck = S // TK
    const = lambda shape: pl.BlockSpec(shape, lambda b, i: (0,) * len(shape))
    rope_a = pl.BlockSpec((GQA_HEAD_DIM // 2, tm), lambda b, i: (0, i))
    rope_b = pl.BlockSpec((MLA_ROPE_DIM // 2, tm), lambda b, i: (0, i))
    out_shape = (
        jax.ShapeDtypeStruct((B, GQA_HEADS, QK_PAD, S), BF16),
        jax.ShapeDtypeStruct((B, 1, S, QK_PAD), BF16),
        jax.ShapeDtypeStruct((B, GQA_KV_HEADS, nck, HEAD_V_EXT, TK), BF16),
        jax.ShapeDtypeStruct((B, MLA_HEADS, QK_PAD, S), BF16),
        jax.ShapeDtypeStruct((B, MLA_HEADS, S, QK_PAD), BF16),
        jax.ShapeDtypeStruct((B, MLA_HEADS, nck, HEAD_V_EXT, TK), BF16),
    )
    out_specs = (
        pl.BlockSpec((1, GQA_HEADS, QK_PAD, tm), lambda b, i: (b, 0, 0, i)),
        pl.BlockSpec((1, 1, tm, QK_PAD), lambda b, i: (b, 0, i, 0)),
        pl.BlockSpec((1, GQA_KV_HEADS, cpt, HEAD_V_EXT, TK), lambda b, i: (b, 0, i, 0, 0)),
        pl.BlockSpec((1, MLA_HEADS, QK_PAD, tm), lambda b, i: (b, 0, 0, i)),
        pl.BlockSpec((1, MLA_HEADS, tm, QK_PAD), lambda b, i: (b, 0, i, 0)),
        pl.BlockSpec((1, MLA_HEADS, cpt, HEAD_V_EXT, TK), lambda b, i: (b, 0, i, 0, 0)),
    )
    return pl.pallas_call(
        functools.partial(_pre_kernel, qscale_m=MLA_QK_DIM ** -0.5 * LOG2E),
        grid=(B, nt),
        in_specs=[
            pl.BlockSpec((1, tm, D_MODEL), lambda b, i: (b, i, 0)),
            pl.BlockSpec((1, 6, D_MODEL), lambda b, i: (b, 0, 0)),
            const((IN_WIDTH, D_MODEL)),
            const((MLA_HEADS * MLA_QK_DIM, MLA_Q_RANK)),
            const((MLA_HEADS * (MLA_NOPE_DIM + MLA_V_DIM), MLA_KV_RANK)),
            const((GQA_HEAD_DIM, 1)),
            const((GQA_HEAD_DIM, 1)),
            const((MLA_Q_RANK, 1)),
            const((MLA_KV_RANK, 1)),
            rope_a, rope_a, rope_b, rope_b,
        ],
        out_specs=out_specs,
        out_shape=out_shape,
        compiler_params=pltpu.CompilerParams(
            dimension_semantics=("parallel", "parallel"),
            vmem_limit_bytes=VMEM_LIMIT),
        name="pre_attention",
    )(x, mod, prep["w_in_t"], prep["w_uq_t"], prep["w_ukv_t"],
      prep["gq_col"], prep["gk_col"], prep["gql_col"], prep["gkvl_col"],
      ropes["cos_a"], ropes["sin_a"], ropes["cos_b"], ropes["sin_b"])


def _attn_kernel(q_ref, k_ref, v_ref, o_ref, s0_ref, s1_ref, cmax0_ref, cmax1_ref, acc_ref,
                 *, n_chunks):
    group = q_ref.shape[2]

    def lanes(i):
        t = i // group
        return slice(t * STREAM_W, (t + 1) * STREAM_W)

    def q_stream(i):
        return q_ref[0, 0, i % group, :, lanes(i)]

    def load_k(c):
        off = pl.multiple_of(c * TK, TK)
        return k_ref[0, 0, pl.ds(off, TK), :]

    def qk(kc, slot, i):
        s_ref, cmax_ref = slot
        s = jnp.dot(kc, q_stream(i), preferred_element_type=F32)
        s_ref[i] = s
        cmax_ref[i] = jnp.max(s, axis=0, keepdims=True)

    def scores(c, slot):
        kc = load_k(c)
        for i in range(N_STREAMS):
            qk(kc, slot, i)

    def softmax(slot, i, m_old):
        s_ref, cmax_ref = slot
        m_new = jnp.maximum(m_old, cmax_ref[i])
        p = jnp.exp2(s_ref[i] - m_new).astype(BF16)
        return m_new, jnp.exp2(m_old - m_new), p

    def pv(c, i, alpha, p):
        acc_ref[i] = alpha * acc_ref[i] + jnp.dot(v_ref[0, 0, c], p, preferred_element_type=F32)

    slots = ((s0_ref, cmax0_ref), (s1_ref, cmax1_ref))

    def trip(c0, ms, prefetch_next_trip):
        ms = list(ms)
        for u in range(CHUNKS_PER_TRIP):
            c = c0 + u
            produce = u + 1 < CHUNKS_PER_TRIP or prefetch_next_trip
            if produce:
                kc = load_k(c + 1)
            for i in range(N_STREAMS):
                if produce:
                    qk(kc, slots[(u + 1) % 2], i)
                ms[i], alpha, p = softmax(slots[u % 2], i, ms[i])
                pv(c, i, alpha, p)
        return tuple(ms)

    acc_ref[...] = jnp.zeros(acc_ref.shape, F32)
    ms = tuple(jnp.full((1, STREAM_W), -jnp.inf, F32) for _ in range(N_STREAMS))
    scores(0, slots[0])
    ms = lax.fori_loop(0, n_chunks // CHUNKS_PER_TRIP - 1,
                       lambda t, ms: trip(t * CHUNKS_PER_TRIP, ms, True), ms)
    trip(n_chunks - CHUNKS_PER_TRIP, ms, False)
    for i in range(N_STREAMS):
        acc = acc_ref[i]
        o = acc[:HEAD_V] * (1.0 / acc[HEAD_V:HEAD_V + 1])
        o_ref[0, 0, i % group, :, lanes(i)] = o


def _attention(q_t, k, v_t, grouped):
    B, heads, _, S = q_t.shape
    n_chunks = S // TK
    assert n_chunks % CHUNKS_PER_TRIP == 0
    group = GQA_GROUP if grouped else 1
    kv = heads // group
    tq = N_STREAMS // group * STREAM_W
    q5 = q_t.reshape(B, kv, group, QK_PAD, S)
    q_spec = pl.BlockSpec((1, 1, group, QK_PAD, tq), lambda b, j, i: (b, j, 0, 0, i))
    k_map = (lambda b, j, i: (b, 0, 0, 0)) if grouped else (lambda b, j, i: (b, j, 0, 0))
    o_shape = (B, kv, group, HEAD_V, S)
    o_spec = pl.BlockSpec((1, 1, group, HEAD_V, tq), lambda b, j, i: (b, j, 0, 0, i))
    nq = S // tq
    out = pl.pallas_call(
        functools.partial(_attn_kernel, n_chunks=n_chunks),
        grid=(B, kv, nq),
        in_specs=[
            q_spec,
            pl.BlockSpec((1, 1, S, QK_PAD), k_map),
            pl.BlockSpec((1, 1, n_chunks, HEAD_V_EXT, TK), lambda b, j, i: (b, j, 0, 0, 0)),
        ],
        out_specs=o_spec,
        out_shape=jax.ShapeDtypeStruct(o_shape, F32),
        scratch_shapes=[
            pltpu.VMEM((N_STREAMS, TK, STREAM_W), F32),
            pltpu.VMEM((N_STREAMS, TK, STREAM_W), F32),
            pltpu.VMEM((N_STREAMS, 1, STREAM_W), F32),
            pltpu.VMEM((N_STREAMS, 1, STREAM_W), F32),
            pltpu.VMEM((N_STREAMS, HEAD_V_EXT, STREAM_W), F32),
        ],
        compiler_params=pltpu.CompilerParams(
            dimension_semantics=("parallel", "parallel", "parallel"),
            vmem_limit_bytes=VMEM_LIMIT),
        name="attention_grouped" if grouped else "attention_single",
    )(q5, k, v_t)
    return out.reshape(B, heads * HEAD_V, S)


def _post_kernel(x_ref, oa_ref, om_ref, mod_ref, ga_ref, gb_ref, w_out_ref,
                 ln1g_ref, ln1b_ref, w_gate_ref, w_up_ref, w_down_ref,
                 ln2g_ref, ln2b_ref, y_ref):
    x = x_ref[0]
    mod = mod_ref[0]
    g1, sh2, sc2, g2 = mod[2:3], mod[3:4], mod[4:5], mod[5:6]
    na = _rms_cols(oa_ref[0], ga_ref[...])
    nb = _rms_cols(om_ref[0], gb_ref[...])
    cat_t = jnp.concatenate([na, nb], axis=0).astype(BF16)
    mix = lax.dot_general(cat_t, w_out_ref[...], (((0,), (0,)), ((), ())),
                          preferred_element_type=F32)
    x1 = _layer_norm_rows(ALPHA * x + g1 * mix) * ln1g_ref[...] + ln1b_ref[...]
    u2 = (_layer_norm_rows(x1) * (1.0 + sc2) + sh2).astype(BF16)
    f = None
    for lo, hi in zip(FF_BOUNDS[:-1], FF_BOUNDS[1:]):
        cols = slice(lo, hi)
        gate = jnp.dot(u2, w_gate_ref[:, cols], preferred_element_type=F32)
        up = jnp.dot(u2, w_up_ref[:, cols], preferred_element_type=F32)
        act = (gate * (1.0 / (1.0 + jnp.exp(-gate))) * up).astype(BF16)
        part = jnp.dot(act, w_down_ref[cols, :], preferred_element_type=F32)
        f = part if f is None else f + part
    y_ref[0] = _layer_norm_rows(ALPHA * x1 + g2 * f) * ln2g_ref[...] + ln2b_ref[...]


def _post_attention(x, o_a, o_m, mod, prep):
    B, S, _ = x.shape
    tm = TM_POST
    single = pl.Buffered(1)
    const = lambda shape: pl.BlockSpec(shape, lambda b, i: (0,) * len(shape),
                                       pipeline_mode=single)
    return pl.pallas_call(
        _post_kernel,
        grid=(B, S // tm),
        in_specs=[
            pl.BlockSpec((1, tm, D_MODEL), lambda b, i: (b, i, 0)),
            pl.BlockSpec((1, GQA_WIDTH, tm), lambda b, i: (b, 0, i)),
            pl.BlockSpec((1, MLA_WIDTH, tm), lambda b, i: (b, 0, i)),
            pl.BlockSpec((1, 6, D_MODEL), lambda b, i: (b, 0, 0)),
            const((GQA_WIDTH, 1)),
            const((MLA_WIDTH, 1)),
            const((MIX_WIDTH, D_MODEL)),
            const((1, D_MODEL)),
            const((1, D_MODEL)),
            const((D_MODEL, D_FF)),
            const((D_MODEL, D_FF)),
            const((D_FF, D_MODEL)),
            const((1, D_MODEL)),
            const((1, D_MODEL)),
        ],
        out_specs=pl.BlockSpec((1, tm, D_MODEL), lambda b, i: (b, i, 0)),
        out_shape=jax.ShapeDtypeStruct((B, S, D_MODEL), F32),
        compiler_params=pltpu.CompilerParams(
            dimension_semantics=("parallel", "parallel"),
            vmem_limit_bytes=VMEM_LIMIT),
        name="post_attention",
    )(x, o_a, o_m, mod, prep["ga_col"], prep["gb_col"], prep["w_out"],
      prep["ln1_g"], prep["ln1_b"], prep["w_gate"], prep["w_up"], prep["w_down"],
      prep["ln2_g"], prep["ln2_b"])


def _rope_tables(seq_len, rot_dim):
    rows = seq_len // GRID_W
    row = jnp.repeat(jnp.arange(rows, dtype=F32), GRID_W)
    col = jnp.tile(jnp.arange(GRID_W, dtype=F32), rows)
    n = rot_dim // 4
    inv_freq = 1.0 / jnp.power(ROPE_THETA, jnp.arange(n, dtype=F32) / n)
    ang = jnp.concatenate([inv_freq[:, None] * row[None, :],
                           inv_freq[:, None] * col[None, :]], axis=0)
    return jnp.cos(ang), jnp.sin(ang)


def _trunk(x, mod, prep):
    S = x.shape[1]
    cos_a, sin_a = _rope_tables(S, GQA_HEAD_DIM)
    cos_b, sin_b = _rope_tables(S, MLA_ROPE_DIM)
    ropes = dict(cos_a=cos_a, sin_a=sin_a, cos_b=cos_b, sin_b=sin_b)
    qa_t, k_a, va_t, qm_t, k_m, vm_t = _pre_attention(x, mod, prep, ropes)
    o_a = _attention(qa_t, k_a, va_t, True)
    o_m = _attention(qm_t, k_m, vm_t, False)
    return _post_attention(x, o_a, o_m, mod, prep)


def kernel(x_prompt, x_sample, c_prompt, c_sample, w_ada, b_ada, w_in, gqa_q_norm, gqa_k_norm,
           mla_q_norm, w_uq, mla_kv_norm, w_ukv, grp_norm_a, grp_norm_b, w_out,
           ln1_g, ln1_b, w_gate, w_up, w_down, ln2_g, ln2_b):
    l = 0
    col = lambda v: v.reshape(-1, 1).astype(F32)
    row = lambda v: v.reshape(1, -1).astype(F32)
    prep = dict(
        w_in_t=w_in[l].T.astype(BF16),
        w_uq_t=w_uq[l].T.astype(BF16),
        w_ukv_t=w_ukv[l].T.astype(BF16),
        gq_col=col(gqa_q_norm[l] * (GQA_HEAD_DIM ** -0.5 * LOG2E)),
        gk_col=col(gqa_k_norm[l]),
        gql_col=col(mla_q_norm[l]),
        gkvl_col=col(mla_kv_norm[l]),
        ga_col=col(grp_norm_a[l]),
        gb_col=col(grp_norm_b[l]),
        w_out=w_out[l].astype(BF16),
        ln1_g=row(ln1_g[l]), ln1_b=row(ln1_b[l]),
        w_gate=w_gate[l].astype(BF16),
        w_up=w_up[l].astype(BF16),
        w_down=w_down[l].astype(BF16),
        ln2_g=row(ln2_g[l]), ln2_b=row(ln2_b[l]),
    )
    nb = c_prompt.shape[0]
    mod = _ada_mod(jnp.concatenate([c_prompt, c_sample], axis=0), w_ada[l], b_ada[l])
    mod = mod.reshape(-1, 6, D_MODEL)
    y_prompt = _trunk(x_prompt, mod[:nb], prep)
    y_sample = _trunk(x_sample, mod[nb:], prep)
    return (y_prompt, y_sample)
```

```python
import functools
import math

import jax
import jax.numpy as jnp
from jax import lax
from jax.experimental import pallas as pl
from jax.experimental.pallas import tpu as pltpu

D_MODEL = 1024
DEPTH = 1
GRID_W = 64
ROPE_THETA = 10000.0
NORM_EPS = 1e-6
GQA_HEADS = 8
GQA_KV_HEADS = 2
GQA_GROUP = GQA_HEADS // GQA_KV_HEADS
GQA_HEAD_DIM = 64
MLA_HEADS = 8
MLA_Q_RANK = 256
MLA_KV_RANK = 128
MLA_NOPE_DIM = 64
MLA_ROPE_DIM = 32
MLA_QK_DIM = MLA_NOPE_DIM + MLA_ROPE_DIM
MLA_V_DIM = 64
GQA_WIDTH = GQA_HEADS * GQA_HEAD_DIM
MLA_WIDTH = MLA_HEADS * MLA_V_DIM
MIX_WIDTH = GQA_WIDTH + MLA_WIDTH
GQA_KV_WIDTH = GQA_KV_HEADS * GQA_HEAD_DIM
IN_WIDTH = GQA_WIDTH + 2 * GQA_KV_WIDTH + MLA_Q_RANK + MLA_KV_RANK + MLA_ROPE_DIM
D_FF = -(-8 * D_MODEL // (3 * 256)) * 256
ALPHA = (2.0 * DEPTH) ** 0.25

OFF_QA = 0
OFF_KA = OFF_QA + GQA_WIDTH
OFF_VA = OFF_KA + GQA_KV_WIDTH
OFF_QL = OFF_VA + GQA_KV_WIDTH
OFF_KVL = OFF_QL + MLA_Q_RANK
OFF_KR = OFF_KVL + MLA_KV_RANK

QK_PAD = 128
HEAD_V = 64
ONES_ROWS = 16
HEAD_V_EXT = HEAD_V + ONES_ROWS
LOG2E = math.log2(math.e)

TM_PRE = 512
TK = 512
CHUNKS_PER_TRIP = 4
N_STREAMS = 16
STREAM_W = 256
TM_POST = 512
MXU_TILE = 256
FF_BOUNDS = (0, (D_FF // MXU_TILE + 1) // 2 * MXU_TILE, D_FF)
VMEM_LIMIT = 56 * 1024 * 1024

F32 = jnp.float32
BF16 = jnp.bfloat16


def _layer_norm_rows(x):
    mu = jnp.mean(x, axis=-1, keepdims=True)
    xc = x - mu
    var = jnp.mean(xc * xc, axis=-1, keepdims=True)
    return xc * lax.rsqrt(var + NORM_EPS)


def _rms_cols(xt, gain_col):
    ms = jnp.mean(xt * xt, axis=0, keepdims=True)
    return xt * lax.rsqrt(ms + NORM_EPS) * gain_col


def _rope_cols(xt, cos_t, sin_t):
    half = xt.shape[0] // 2
    x1, x2 = xt[:half], xt[half:]
    return jnp.concatenate([x1 * cos_t - x2 * sin_t, x1 * sin_t + x2 * cos_t], axis=0)


def _ada_kernel(c_ref, w_ref, b_ref, o_ref):
    c = c_ref[...]
    s = c * (1.0 / (1.0 + jnp.exp(-c)))
    o_ref[...] = jnp.dot(s, w_ref[...], preferred_element_type=F32) + b_ref[...]


def _ada_mod(c, w_ada, b_ada):
    n = c.shape[0]
    cols = 6 * D_MODEL
    return pl.pallas_call(
        _ada_kernel,
        grid=(6,),
        in_specs=[
            pl.BlockSpec((n, D_MODEL), lambda j: (0, 0)),
            pl.BlockSpec((D_MODEL, D_MODEL), lambda j: (0, j)),
            pl.BlockSpec((1, D_MODEL), lambda j: (0, j)),
        ],
        out_specs=pl.BlockSpec((n, D_MODEL), lambda j: (0, j)),
        out_shape=jax.ShapeDtypeStruct((n, cols), F32),
        name="ada_mod",
    )(c, w_ada, b_ada.reshape(1, cols))


def _pre_kernel(x_ref, mod_ref, w_in_ref, w_uq_ref, w_ukv_ref,
                gq_ref, gk_ref, gql_ref, gkvl_ref,
                cosa_ref, sina_ref, cosb_ref, sinb_ref,
                qa_ref, ka_ref, va_ref, qm_ref, km_ref, vm_ref, *, qscale_m):
    x = x_ref[0]
    mod = mod_ref[0]
    sh1, sc1 = mod[0:1], mod[1:2]
    u = (_layer_norm_rows(x) * (1.0 + sc1) + sh1).astype(BF16)
    ht = lax.dot_general(w_in_ref[...], u, (((1,), (1,)), ((), ())),
                         preferred_element_type=F32)
    tm = ht.shape[1]
    n_chunks = tm // TK
    cos_a, sin_a = cosa_ref[...], sina_ref[...]
    cos_b, sin_b = cosb_ref[...], sinb_ref[...]
    zeros_half = jnp.zeros((GQA_HEAD_DIM, tm), F32)
    ones_rows = jnp.where(lax.broadcasted_iota(jnp.int32, (ONES_ROWS, tm), 0) == 0, 1.0, 0.0)

    gq = gq_ref[...]
    for h in range(GQA_HEADS):
        slab = ht[OFF_QA + h * GQA_HEAD_DIM: OFF_QA + (h + 1) * GQA_HEAD_DIM]
        q = _rope_cols(_rms_cols(slab, gq), cos_a, sin_a)
        parts = [zeros_half] * GQA_KV_HEADS
        parts[h // GQA_GROUP] = q
        qa_ref[0, h] = jnp.concatenate(parts, axis=0).astype(BF16)

    gk = gk_ref[...]
    k_parts = []
    for j in range(GQA_KV_HEADS):
        slab = ht[OFF_KA + j * GQA_HEAD_DIM: OFF_KA + (j + 1) * GQA_HEAD_DIM]
        k_parts.append(_rope_cols(_rms_cols(slab, gk), cos_a, sin_a))
        v = ht[OFF_VA + j * GQA_HEAD_DIM: OFF_VA + (j + 1) * GQA_HEAD_DIM]
        v = jnp.concatenate([v, ones_rows], axis=0).astype(BF16)
        for c in range(n_chunks):
            va_ref[0, j, c] = v[:, c * TK:(c + 1) * TK]
    ka_ref[0, 0] = jnp.concatenate(k_parts, axis=0).T.astype(BF16)

    ql = _rms_cols(ht[OFF_QL: OFF_QL + MLA_Q_RANK], gql_ref[...]).astype(BF16)
    qmt = jnp.dot(w_uq_ref[...], ql, preferred_element_type=F32)
    zeros_pad = jnp.zeros((QK_PAD - MLA_QK_DIM, tm), F32)
    for h in range(MLA_HEADS):
        base = h * MLA_QK_DIM
        nope = qmt[base: base + MLA_NOPE_DIM]
        pe = _rope_cols(qmt[base + MLA_NOPE_DIM: base + MLA_QK_DIM], cos_b, sin_b)
        q = jnp.concatenate([nope, pe], axis=0) * qscale_m
        qm_ref[0, h] = jnp.concatenate([q, zeros_pad], axis=0).astype(BF16)

    kvl = _rms_cols(ht[OFF_KVL: OFF_KVL + MLA_KV_RANK], gkvl_ref[...]).astype(BF16)
    kvt = jnp.dot(w_ukv_ref[...], kvl, preferred_element_type=F32)
    k_pe = _rope_cols(ht[OFF_KR: OFF_KR + MLA_ROPE_DIM], cos_b, sin_b)
    for h in range(MLA_HEADS):
        base = h * (MLA_NOPE_DIM + MLA_V_DIM)
        k_nope = kvt[base: base + MLA_NOPE_DIM]
        v = kvt[base + MLA_NOPE_DIM: base + MLA_NOPE_DIM + MLA_V_DIM]
        v = jnp.concatenate([v, ones_rows], axis=0).astype(BF16)
        for c in range(n_chunks):
            vm_ref[0, h, c] = v[:, c * TK:(c + 1) * TK]
        kt = jnp.concatenate([k_nope, k_pe, zeros_pad], axis=0)
        km_ref[0, h] = kt.T.astype(BF16)


def _pre_attention(x, mod, prep, ropes):
    B, S, _ = x.shape
    tm = TM_PRE
    nt = S // tm
    cpt = tm // TK
    nck = S // TK
    const = lambda shape: pl.BlockSpec(shape, lambda b, i: (0,) * len(shape))
    rope_a = pl.BlockSpec((GQA_HEAD_DIM // 2, tm), lambda b, i: (0, i))
    rope_b = pl.BlockSpec((MLA_ROPE_DIM // 2, tm), lambda b, i: (0, i))
    out_shape = (
        jax.ShapeDtypeStruct((B, GQA_HEADS, QK_PAD, S), BF16),
        jax.ShapeDtypeStruct((B, 1, S, QK_PAD), BF16),
        jax.ShapeDtypeStruct((B, GQA_KV_HEADS, nck, HEAD_V_EXT, TK), BF16),
        jax.ShapeDtypeStruct((B, MLA_HEADS, QK_PAD, S), BF16),
        jax.ShapeDtypeStruct((B, MLA_HEADS, S, QK_PAD), BF16),
        jax.ShapeDtypeStruct((B, MLA_HEADS, nck, HEAD_V_EXT, TK), BF16),
    )
    out_specs = (
        pl.BlockSpec((1, GQA_HEADS, QK_PAD, tm), lambda b, i: (b, 0, 0, i)),
        pl.BlockSpec((1, 1, tm, QK_PAD), lambda b, i: (b, 0, i, 0)),
        pl.BlockSpec((1, GQA_KV_HEADS, cpt, HEAD_V_EXT, TK), lambda b, i: (b, 0, i, 0, 0)),
        pl.BlockSpec((1, MLA_HEADS, QK_PAD, tm), lambda b, i: (b, 0, 0, i)),
        pl.BlockSpec((1, MLA_HEADS, tm, QK_PAD), lambda b, i: (b, 0, i, 0)),
        pl.BlockSpec((1, MLA_HEADS, cpt, HEAD_V_EXT, TK), lambda b, i: (b, 0, i, 0, 0)),
    )
    return pl.pallas_call(
        functools.partial(_pre_kernel, qscale_m=MLA_QK_DIM ** -0.5 * LOG2E),
        grid=(B, nt),
        in_specs=[
            pl.BlockSpec((1, tm, D_MODEL), lambda b, i: (b, i, 0)),
            pl.BlockSpec((1, 6, D_MODEL), lambda b, i: (b, 0, 0)),
            const((IN_WIDTH, D_MODEL)),
            const((MLA_HEADS * MLA_QK_DIM, MLA_Q_RANK)),
            const((MLA_HEADS * (MLA_NOPE_DIM + MLA_V_DIM), MLA_KV_RANK)),
            const((GQA_HEAD_DIM, 1)),
            const((GQA_HEAD_DIM, 1)),
            const((MLA_Q_RANK, 1)),
            const((MLA_KV_RANK, 1)),
            rope_a, rope_a, rope_b, rope_b,
        ],
        out_specs=out_specs,
        out_shape=out_shape,
        compiler_params=pltpu.CompilerParams(
            dimension_semantics=("parallel", "parallel"),
            vmem_limit_bytes=VMEM_LIMIT),
        name="pre_attention",
    )(x, mod, prep["w_in_t"], prep["w_uq_t"], prep["w_ukv_t"],
      prep["gq_col"], prep["gk_col"], prep["gql_col"], prep["gkvl_col"],
      ropes["cos_a"], ropes["sin_a"], ropes["cos_b"], ropes["sin_b"])


def _attn_kernel(q_ref, k_ref, v_ref, o_ref, s0_ref, s1_ref, cmax0_ref, cmax1_ref, acc_ref,
                 *, n_chunks):
    group = q_ref.shape[2]
    n_streams = acc_ref.shape[0]

    def lanes(i):
        t = i // group
        return slice(t * STREAM_W, (t + 1) * STREAM_W)

    def q_stream(i):
        return q_ref[0, 0, i % group, :, lanes(i)]

    def load_k(c):
        off = pl.multiple_of(c * TK, TK)
        return k_ref[0, 0, pl.ds(off, TK), :]

    def qk(kc, slot, i):
        s_ref, cmax_ref = slot
        s = jnp.dot(kc, q_stream(i), preferred_element_type=F32)
        s_ref[i] = s
        cmax_ref[i] = jnp.max(s, axis=0, keepdims=True)

    def scores(c, slot):
        kc = load_k(c)
        for i in range(n_streams):
            qk(kc, slot, i)

    def softmax(slot, i, m_old):
        s_ref, cmax_ref = slot
        m_new = jnp.maximum(m_old, cmax_ref[i])
        p = jnp.exp2(s_ref[i] - m_new).astype(BF16)
        return m_new, jnp.exp2(m_old - m_new), p

    def pv(c, i, alpha, p):
        acc_ref[i] = alpha * acc_ref[i] + jnp.dot(v_ref[0, 0, c], p, preferred_element_type=F32)

    slots = ((s0_ref, cmax0_ref), (s1_ref, cmax1_ref))

    def trip(c0, ms, prefetch_next_trip):
        ms = list(ms)
        for u in range(CHUNKS_PER_TRIP):
            c = c0 + u
            produce = u + 1 < CHUNKS_PER_TRIP or prefetch_next_trip
            if produce:
                kc = load_k(c + 1)
            for i in range(n_streams):
                if produce:
                    qk(kc, slots[(u + 1) % 2], i)
                ms[i], alpha, p = softmax(slots[u % 2], i, ms[i])
                pv(c, i, alpha, p)
        return tuple(ms)

    acc_ref[...] = jnp.zeros(acc_ref.shape, F32)
    ms = tuple(jnp.full((1, STREAM_W), -jnp.inf, F32) for _ in range(n_streams))
    scores(0, slots[0])
    ms = lax.fori_loop(0, n_chunks // CHUNKS_PER_TRIP - 1,
                       lambda t, ms: trip(t * CHUNKS_PER_TRIP, ms, True), ms)
    trip(n_chunks - CHUNKS_PER_TRIP, ms, False)
    for i in range(n_streams):
        acc = acc_ref[i]
        o = acc[:HEAD_V] * (1.0 / acc[HEAD_V:HEAD_V + 1])
        o_ref[0, 0, i % group, :, lanes(i)] = o


def _attention(q_t, k, v_t, grouped):
    B, heads, _, S = q_t.shape
    n_chunks = S // TK
    assert n_chunks % CHUNKS_PER_TRIP == 0
    group = GQA_GROUP if grouped else 1
    kv = heads // group
    n_streams = min(N_STREAMS, S // STREAM_W * group)
    tq = n_streams // group * STREAM_W
    q5 = q_t.reshape(B, kv, group, QK_PAD, S)
    q_spec = pl.BlockSpec((1, 1, group, QK_PAD, tq), lambda b, j, i: (b, j, 0, 0, i))
    k_map = (lambda b, j, i: (b, 0, 0, 0)) if grouped else (lambda b, j, i: (b, j, 0, 0))
    o_shape = (B, kv, group, HEAD_V, S)
    o_spec = pl.BlockSpec((1, 1, group, HEAD_V, tq), lambda b, j, i: (b, j, 0, 0, i))
    nq = S // tq
    out = pl.pallas_call(
        functools.partial(_attn_kernel, n_chunks=n_chunks),
        grid=(B, kv, nq),
        in_specs=[
            q_spec,
            pl.BlockSpec((1, 1, S, QK_PAD), k_map),
            pl.BlockSpec((1, 1, n_chunks, HEAD_V_EXT, TK), lambda b, j, i: (b, j, 0, 0, 0)),
        ],
        out_specs=o_spec,
        out_shape=jax.ShapeDtypeStruct(o_shape, F32),
        scratch_shapes=[
            pltpu.VMEM((n_streams, TK, STREAM_W), F32),
            pltpu.VMEM((n_streams, TK, STREAM_W), F32),
            pltpu.VMEM((n_streams, 1, STREAM_W), F32),
            pltpu.VMEM((n_streams, 1, STREAM_W), F32),
            pltpu.VMEM((n_streams, HEAD_V_EXT, STREAM_W), F32),
        ],
        compiler_params=pltpu.CompilerParams(
            dimension_semantics=("parallel", "parallel", "parallel"),
            vmem_limit_bytes=VMEM_LIMIT),
        name="attention_grouped" if grouped else "attention_single",
    )(q5, k, v_t)
    return out.reshape(B, heads * HEAD_V, S)


def _post_kernel(x_ref, oa_ref, om_ref, mod_ref, ga_ref, gb_ref, w_out_ref,
                 ln1g_ref, ln1b_ref, w_gate_ref, w_up_ref, w_down_ref,
                 ln2g_ref, ln2b_ref, y_ref):
    x = x_ref[0]
    mod = mod_ref[0]
    g1, sh2, sc2, g2 = mod[2:3], mod[3:4], mod[4:5], mod[5:6]
    na = _rms_cols(oa_ref[0], ga_ref[...])
    nb = _rms_cols(om_ref[0], gb_ref[...])
    cat_t = jnp.concatenate([na, nb], axis=0).astype(BF16)
    mix = lax.dot_general(cat_t, w_out_ref[...], (((0,), (0,)), ((), ())),
                          preferred_element_type=F32)
    x1 = _layer_norm_rows(ALPHA * x + g1 * mix) * ln1g_ref[...] + ln1b_ref[...]
    u2 = (_layer_norm_rows(x1) * (1.0 + sc2) + sh2).astype(BF16)
    f = None
    for lo, hi in zip(FF_BOUNDS[:-1], FF_BOUNDS[1:]):
        cols = slice(lo, hi)
        gate = jnp.dot(u2, w_gate_ref[:, cols], preferred_element_type=F32)
        up = jnp.dot(u2, w_up_ref[:, cols], preferred_element_type=F32)
        act = (gate * (1.0 / (1.0 + jnp.exp(-gate))) * up).astype(BF16)
        part = jnp.dot(act, w_down_ref[cols, :], preferred_element_type=F32)
        f = part if f is None else f + part
    y_ref[0] = _layer_norm_rows(ALPHA * x1 + g2 * f) * ln2g_ref[...] + ln2b_ref[...]


def _post_attention(x, o_a, o_m, mod, prep):
    B, S, _ = x.shape
    tm = TM_POST
    single = pl.Buffered(1)
    const = lambda shape: pl.BlockSpec(shape, lambda b, i: (0,) * len(shape),
                                       pipeline_mode=single)
    return pl.pallas_call(
        _post_kernel,
        grid=(B, S // tm),
        in_specs=[
            pl.BlockSpec((1, tm, D_MODEL), lambda b, i: (b, i, 0)),
            pl.BlockSpec((1, GQA_WIDTH, tm), lambda b, i: (b, 0, i)),
            pl.BlockSpec((1, MLA_WIDTH, tm), lambda b, i: (b, 0, i)),
            pl.BlockSpec((1, 6, D_MODEL), lambda b, i: (b, 0, 0)),
            const((GQA_WIDTH, 1)),
            const((MLA_WIDTH, 1)),
            const((MIX_WIDTH, D_MODEL)),
            const((1, D_MODEL)),
            const((1, D_MODEL)),
            const((D_MODEL, D_FF)),
            const((D_MODEL, D_FF)),
            const((D_FF, D_MODEL)),
            const((1, D_MODEL)),
            const((1, D_MODEL)),
        ],
        out_specs=pl.BlockSpec((1, tm, D_MODEL), lambda b, i: (b, i, 0)),
        out_shape=jax.ShapeDtypeStruct((B, S, D_MODEL), F32),
        compiler_params=pltpu.CompilerParams(
            dimension_semantics=("parallel", "parallel"),
            vmem_limit_bytes=VMEM_LIMIT),
        name="post_attention",
    )(x, o_a, o_m, mod, prep["ga_col"], prep["gb_col"], prep["w_out"],
      prep["ln1_g"], prep["ln1_b"], prep["w_gate"], prep["w_up"], prep["w_down"],
      prep["ln2_g"], prep["ln2_b"])


def _rope_tables(seq_len, rot_dim):
    rows = seq_len // GRID_W
    row = jnp.repeat(jnp.arange(rows, dtype=F32), GRID_W)
    col = jnp.tile(jnp.arange(GRID_W, dtype=F32), rows)
    n = rot_dim // 4
    inv_freq = 1.0 / jnp.power(ROPE_THETA, jnp.arange(n, dtype=F32) / n)
    ang = jnp.concatenate([inv_freq[:, None] * row[None, :],
                           inv_freq[:, None] * col[None, :]], axis=0)
    return jnp.cos(ang), jnp.sin(ang)


def _trunk(x, mod, prep):
    S = x.shape[1]
    cos_a, sin_a = _rope_tables(S, GQA_HEAD_DIM)
    cos_b, sin_b = _rope_tables(S, MLA_ROPE_DIM)
    ropes = dict(cos_a=cos_a, sin_a=sin_a, cos_b=cos_b, sin_b=sin_b)
    qa_t, k_a, va_t, qm_t, k_m, vm_t = _pre_attention(x, mod, prep, ropes)
    o_a = _attention(qa_t, k_a, va_t, True)
    o_m = _attention(qm_t, k_m, vm_t, False)
    return _post_attention(x, o_a, o_m, mod, prep)


def kernel(x_prompt, x_sample, c_prompt, c_sample, w_ada, b_ada, w_in, gqa_q_norm, gqa_k_norm,
           mla_q_norm, w_uq, mla_kv_norm, w_ukv, grp_norm_a, grp_norm_b, w_out,
           ln1_g, ln1_b, w_gate, w_up, w_down, ln2_g, ln2_b):
    l = 0
    col = lambda v: v.reshape(-1, 1).astype(F32)
    row = lambda v: v.reshape(1, -1).astype(F32)
    prep = dict(
        w_in_t=w_in[l].T.astype(BF16),
        w_uq_t=w_uq[l].T.astype(BF16),
        w_ukv_t=w_ukv[l].T.astype(BF16),
        gq_col=col(gqa_q_norm[l] * (GQA_HEAD_DIM ** -0.5 * LOG2E)),
        gk_col=col(gqa_k_norm[l]),
        gql_col=col(mla_q_norm[l]),
        gkvl_col=col(mla_kv_norm[l]),
        ga_col=col(grp_norm_a[l]),
        gb_col=col(grp_norm_b[l]),
        w_out=w_out[l].astype(BF16),
        ln1_g=row(ln1_g[l]), ln1_b=row(ln1_b[l]),
        w_gate=w_gate[l].astype(BF16),
        w_up=w_up[l].astype(BF16),
        w_down=w_down[l].astype(BF16),
        ln2_g=row(ln2_g[l]), ln2_b=row(ln2_b[l]),
    )
    nb = c_prompt.shape[0]
    mod = _ada_mod(jnp.concatenate([c_prompt, c_sample], axis=0), w_ada[l], b_ada[l])
    mod = mod.reshape(-1, 6, D_MODEL)
    y_prompt = _trunk(x_prompt, mod[:nb], prep)
    y_sample = _trunk(x_sample, mod[nb:], prep)
    return (y_prompt, y_sample)
```

```python
import functools
import math

import jax
import jax.numpy as jnp
from jax import lax
from jax.experimental import pallas as pl
from jax.experimental.pallas import tpu as pltpu

D_MODEL = 1024
DEPTH = 1
GRID_W = 64
ROPE_THETA = 10000.0
NORM_EPS = 1e-6
GQA_HEADS = 8
GQA_KV_HEADS = 2
GQA_GROUP = GQA_HEADS // GQA_KV_HEADS
GQA_HEAD_DIM = 64
MLA_HEADS = 8
MLA_Q_RANK = 256
MLA_KV_RANK = 128
MLA_NOPE_DIM = 64
MLA_ROPE_DIM = 32
MLA_QK_DIM = MLA_NOPE_DIM + MLA_ROPE_DIM
MLA_V_DIM = 64
GQA_WIDTH = GQA_HEADS * GQA_HEAD_DIM
MLA_WIDTH = MLA_HEADS * MLA_V_DIM
MIX_WIDTH = GQA_WIDTH + MLA_WIDTH
GQA_KV_WIDTH = GQA_KV_HEADS * GQA_HEAD_DIM
IN_WIDTH = GQA_WIDTH + 2 * GQA_KV_WIDTH + MLA_Q_RANK + MLA_KV_RANK + MLA_ROPE_DIM
D_FF = -(-8 * D_MODEL // (3 * 256)) * 256
ALPHA = (2.0 * DEPTH) ** 0.25

OFF_QA = 0
OFF_KA = OFF_QA + GQA_WIDTH
OFF_VA = OFF_KA + GQA_KV_WIDTH
OFF_QL = OFF_VA + GQA_KV_WIDTH
OFF_KVL = OFF_QL + MLA_Q_RANK
OFF_KR = OFF_KVL + MLA_KV_RANK

QK_PAD = 128
HEAD_V = 64
ONES_ROWS = 16
HEAD_V_EXT = HEAD_V + ONES_ROWS
LOG2E = math.log2(math.e)

TM_PRE = 512
TK = 512
CHUNKS_PER_TRIP = 4
N_STREAMS = 16
STREAM_W = 256
TM_POST = 512
MXU_TILE = 256
FF_BOUNDS = (0, (D_FF // MXU_TILE + 1) // 2 * MXU_TILE, D_FF)
VMEM_LIMIT = 56 * 1024 * 1024

F32 = jnp.float32
BF16 = jnp.bfloat16


def _layer_norm_rows(x):
    mu = jnp.mean(x, axis=-1, keepdims=True)
    xc = x - mu
    var = jnp.mean(xc * xc, axis=-1, keepdims=True)
    return xc * lax.rsqrt(var + NORM_EPS)


def _rms_cols(xt, gain_col):
    ms = jnp.mean(xt * xt, axis=0, keepdims=True)
    return xt * lax.rsqrt(ms + NORM_EPS) * gain_col


def _rope_cols(xt, cos_t, sin_t):
    half = xt.shape[0] // 2
    x1, x2 = xt[:half], xt[half:]
    return jnp.concatenate([x1 * cos_t - x2 * sin_t, x1 * sin_t + x2 * cos_t], axis=0)


def _ada_kernel(c_ref, w_ref, b_ref, o_ref):
    c = c_ref[...]
    s = c * (1.0 / (1.0 + jnp.exp(-c)))
    o_ref[...] = jnp.dot(s, w_ref[...], preferred_element_type=F32) + b_ref[...]


def _ada_mod(c, w_ada, b_ada):
    n = c.shape[0]
    cols = 6 * D_MODEL
    return pl.pallas_call(
        _ada_kernel,
        grid=(6,),
        in_specs=[
            pl.BlockSpec((n, D_MODEL), lambda j: (0, 0)),
            pl.BlockSpec((D_MODEL, D_MODEL), lambda j: (0, j)),
            pl.BlockSpec((1, D_MODEL), lambda j: (0, j)),
        ],
        out_specs=pl.BlockSpec((n, D_MODEL), lambda j: (0, j)),
        out_shape=jax.ShapeDtypeStruct((n, cols), F32),
        name="ada_mod",
    )(c, w_ada, b_ada.reshape(1, cols))


def _pre_kernel(x_ref, mod_ref, w_in_ref, w_uq_ref, w_ukv_ref,
                gq_ref, gk_ref, gql_ref, gkvl_ref,
                cosa_ref, sina_ref, cosb_ref, sinb_ref,
                qa_ref, ka_ref, va_ref, qm_ref, km_ref, vm_ref, *, qscale_m):
    x = x_ref[0]
    mod = mod_ref[0]
    sh1, sc1 = mod[0:1], mod[1:2]
    u = (_layer_norm_rows(x) * (1.0 + sc1) + sh1).astype(BF16)
    ht = lax.dot_general(w_in_ref[...], u, (((1,), (1,)), ((), ())),
                         preferred_element_type=F32)
    tm = ht.shape[1]
    n_chunks = tm // TK
    cos_a, sin_a = cosa_ref[...], sina_ref[...]
    cos_b, sin_b = cosb_ref[...], sinb_ref[...]
    zeros_half = jnp.zeros((GQA_HEAD_DIM, tm), F32)
    ones_rows = jnp.where(lax.broadcasted_iota(jnp.int32, (ONES_ROWS, tm), 0) == 0, 1.0, 0.0)

    gq = gq_ref[...]
    for h in range(GQA_HEADS):
        slab = ht[OFF_QA + h * GQA_HEAD_DIM: OFF_QA + (h + 1) * GQA_HEAD_DIM]
        q = _rope_cols(_rms_cols(slab, gq), cos_a, sin_a)
        parts = [zeros_half] * GQA_KV_HEADS
        parts[h // GQA_GROUP] = q
        qa_ref[0, h] = jnp.concatenate(parts, axis=0).astype(BF16)

    gk = gk_ref[...]
    k_parts = []
    for j in range(GQA_KV_HEADS):
        slab = ht[OFF_KA + j * GQA_HEAD_DIM: OFF_KA + (j + 1) * GQA_HEAD_DIM]
        k_parts.append(_rope_cols(_rms_cols(slab, gk), cos_a, sin_a))
        v = ht[OFF_VA + j * GQA_HEAD_DIM: OFF_VA + (j + 1) * GQA_HEAD_DIM]
        v = jnp.concatenate([v, ones_rows], axis=0).astype(BF16)
        for c in range(n_chunks):
            va_ref[0, j, c] = v[:, c * TK:(c + 1) * TK]
    ka_ref[0, 0] = jnp.concatenate(k_parts, axis=0).T.astype(BF16)

    ql = _rms_cols(ht[OFF_QL: OFF_QL + MLA_Q_RANK], gql_ref[...]).astype(BF16)
    qmt = jnp.dot(w_uq_ref[...], ql, preferred_element_type=F32)
    zeros_pad = jnp.zeros((QK_PAD - MLA_QK_DIM, tm), F32)
    for h in range(MLA_HEADS):
        base = h * MLA_QK_DIM
        nope = qmt[base: base + MLA_NOPE_DIM]
        pe = _rope_cols(qmt[base + MLA_NOPE_DIM: base + MLA_QK_DIM], cos_b, sin_b)
        q = jnp.concatenate([nope, pe], axis=0) * qscale_m
        qm_ref[0, h] = jnp.concatenate([q, zeros_pad], axis=0).astype(BF16)

    kvl = _rms_cols(ht[OFF_KVL: OFF_KVL + MLA_KV_RANK], gkvl_ref[...]).astype(BF16)
    kvt = jnp.dot(w_ukv_ref[...], kvl, preferred_element_type=F32)
    k_pe = _rope_cols(ht[OFF_KR: OFF_KR + MLA_ROPE_DIM], cos_b, sin_b)
    for h in range(MLA_HEADS):
        base = h * (MLA_NOPE_DIM + MLA_V_DIM)
        k_nope = kvt[base: base + MLA_NOPE_DIM]
        v = kvt[base + MLA_NOPE_DIM: base + MLA_NOPE_DIM + MLA_V_DIM]
        v = jnp.concatenate([v, ones_rows], axis=0).astype(BF16)
        for c in range(n_chunks):
            vm_ref[0, h, c] = v[:, c * TK:(c + 1) * TK]
        kt = jnp.concatenate([k_nope, k_pe, zeros_pad], axis=0)
        km_ref[0, h] = kt.T.astype(BF16)


def _pre_attention(x, mod, prep, ropes):
    B, S, _ = x.shape
    tm = TM_PRE
    nt = S // tm
    cpt = tm // TK
    nck = S // TK
    const = lambda shape: pl.BlockSpec(shape, lambda b, i: (0,) * len(shape))
    rope_a = pl.BlockSpec((GQA_HEAD_DIM // 2, tm), lambda b, i: (0, i))
    rope_b = pl.BlockSpec((MLA_ROPE_DIM // 2, tm), lambda b, i: (0, i))
    out_shape = (
        jax.ShapeDtypeStruct((B, GQA_HEADS, QK_PAD, S), BF16),
        jax.ShapeDtypeStruct((B, 1, S, QK_PAD), BF16),
        jax.ShapeDtypeStruct((B, GQA_KV_HEADS, nck, HEAD_V_EXT, TK), BF16),
        jax.ShapeDtypeStruct((B, MLA_HEADS, QK_PAD, S), BF16),
        jax.ShapeDtypeStruct((B, MLA_HEADS, S, QK_PAD), BF16),
        jax.ShapeDtypeStruct((B, MLA_HEADS, nck, HEAD_V_EXT, TK), BF16),
    )
    out_specs = (
        pl.BlockSpec((1, GQA_HEADS, QK_PAD, tm), lambda b, i: (b, 0, 0, i)),
        pl.BlockSpec((1, 1, tm, QK_PAD), lambda b, i: (b, 0, i, 0)),
        pl.BlockSpec((1, GQA_KV_HEADS, cpt, HEAD_V_EXT, TK), lambda b, i: (b, 0, i, 0, 0)),
        pl.BlockSpec((1, MLA_HEADS, QK_PAD, tm), lambda b, i: (b, 0, 0, i)),
        pl.BlockSpec((1, MLA_HEADS, tm, QK_PAD), lambda b, i: (b, 0, i, 0)),
        pl.BlockSpec((1, MLA_HEADS, cpt, HEAD_V_EXT, TK), lambda b, i: (b, 0, i, 0, 0)),
    )
    return pl.pallas_call(
        functools.partial(_pre_kernel, qscale_m=MLA_QK_DIM ** -0.5 * LOG2E),
        grid=(B, nt),
        in_specs=[
            pl.BlockSpec((1, tm, D_MODEL), lambda b, i: (b, i, 0)),
            pl.BlockSpec((1, 6, D_MODEL), lambda b, i: (b, 0, 0)),
            const((IN_WIDTH, D_MODEL)),
            const((MLA_HEADS * MLA_QK_DIM, MLA_Q_RANK)),
            const((MLA_HEADS * (MLA_NOPE_DIM + MLA_V_DIM), MLA_KV_RANK)),
            const((GQA_HEAD_DIM, 1)),
            const((GQA_HEAD_DIM, 1)),
            const((MLA_Q_RANK, 1)),
            const((MLA_KV_RANK, 1)),
            rope_a, rope_a, rope_b, rope_b,
        ],
        out_specs=out_specs,
        out_shape=out_shape,
        compiler_params=pltpu.CompilerParams(
            dimension_semantics=("parallel", "parallel"),
            vmem_limit_bytes=VMEM_LIMIT),
        name="pre_attention",
    )(x, mod, prep["w_in_t"], prep["w_uq_t"], prep["w_ukv_t"],
      prep["gq_col"], prep["gk_col"], prep["gql_col"], prep["gkvl_col"],
      ropes["cos_a"], ropes["sin_a"], ropes["cos_b"], ropes["sin_b"])


def _attn_kernel(q_ref, k_ref, v_ref, o_ref, s0_ref, s1_ref, cmax0_ref, cmax1_ref, acc_ref,
                 *, n_chunks):
    group = q_ref.shape[2]
    n_streams = acc_ref.shape[0]

    def lanes(i):
        t = i // group
        return slice(t * STREAM_W, (t + 1) * STREAM_W)

    def q_stream(i):
        return q_ref[0, 0, i % group, :, lanes(i)]

    def load_k(c):
        off = pl.multiple_of(c * TK, TK)
        return k_ref[0, 0, pl.ds(off, TK), :]

    def qk(kc, slot, i):
        s_ref, cmax_ref = slot
        s = jnp.dot(kc, q_stream(i), preferred_element_type=F32)
        s_ref[i] = s
        cmax_ref[i] = jnp.max(s, axis=0, keepdims=True)

    def softmax(slot, i, m_old):
        s_ref, cmax_ref = slot
        m_new = jnp.maximum(m_old, cmax_ref[i])
        p = jnp.exp2(s_ref[i] - m_new).astype(BF16)
        return m_new, jnp.exp2(m_old - m_new), p

    def pv(c, i, alpha, p):
        acc_ref[i] = alpha * acc_ref[i] + jnp.dot(v_ref[0, 0, c], p, preferred_element_type=F32)

    slots = ((s0_ref, cmax0_ref), (s1_ref, cmax1_ref))

    def trip(c0, ms, prefetch_next_trip, first=False):
        ms = list(ms)
        for u in range(CHUNKS_PER_TRIP):
            c = c0 + u
            produce = u + 1 < CHUNKS_PER_TRIP or prefetch_next_trip
            if produce:
                kc = load_k(c + 1)
            if first and u == 0:
                kc_first = load_k(c)
            for i in range(n_streams):
                if first and u == 0:
                    qk(kc_first, slots[0], i)
                if produce:
                    qk(kc, slots[(u + 1) % 2], i)
                ms[i], alpha, p = softmax(slots[u % 2], i, ms[i])
                pv(c, i, alpha, p)
        return tuple(ms)

    acc_ref[...] = jnp.zeros(acc_ref.shape, F32)
    ms = tuple(jnp.full((1, STREAM_W), -jnp.inf, F32) for _ in range(n_streams))
    n_trips = n_chunks // CHUNKS_PER_TRIP
    if n_trips == 1:
        trip(0, ms, False, first=True)
    else:
        ms = trip(0, ms, True, first=True)
        ms = lax.fori_loop(1, n_trips - 1,
                           lambda t, ms: trip(t * CHUNKS_PER_TRIP, ms, True), ms)
        trip(n_chunks - CHUNKS_PER_TRIP, ms, False)
    for i in range(n_streams):
        acc = acc_ref[i]
        o = acc[:HEAD_V] * (1.0 / acc[HEAD_V:HEAD_V + 1])
        o_ref[0, 0, i % group, :, lanes(i)] = o


def _attention(q_t, k, v_t, grouped):
    B, heads, _, S = q_t.shape
    n_chunks = S // TK
    assert n_chunks % CHUNKS_PER_TRIP == 0
    group = GQA_GROUP if grouped else 1
    kv = heads // group
    n_streams = min(N_STREAMS, S // STREAM_W * group)
    tq = n_streams // group * STREAM_W
    q5 = q_t.reshape(B, kv, group, QK_PAD, S)
    q_spec = pl.BlockSpec((1, 1, group, QK_PAD, tq), lambda b, j, i: (b, j, 0, 0, i))
    k_map = (lambda b, j, i: (b, 0, 0, 0)) if grouped else (lambda b, j, i: (b, j, 0, 0))
    o_shape = (B, kv, group, HEAD_V, S)
    o_spec = pl.BlockSpec((1, 1, group, HEAD_V, tq), lambda b, j, i: (b, j, 0, 0, i))
    nq = S // tq
    out = pl.pallas_call(
        functools.partial(_attn_kernel, n_chunks=n_chunks),
        grid=(B, kv, nq),
        in_specs=[
            q_spec,
            pl.BlockSpec((1, 1, S, QK_PAD), k_map),
            pl.BlockSpec((1, 1, n_chunks, HEAD_V_EXT, TK), lambda b, j, i: (b, j, 0, 0, 0)),
        ],
        out_specs=o_spec,
        out_shape=jax.ShapeDtypeStruct(o_shape, F32),
        scratch_shapes=[
            pltpu.VMEM((n_streams, TK, STREAM_W), F32),
            pltpu.VMEM((n_streams, TK, STREAM_W), F32),
            pltpu.VMEM((n_streams, 1, STREAM_W), F32),
            pltpu.VMEM((n_streams, 1, STREAM_W), F32),
            pltpu.VMEM((n_streams, HEAD_V_EXT, STREAM_W), F32),
        ],
        compiler_params=pltpu.CompilerParams(
            dimension_semantics=("parallel", "parallel", "parallel"),
            vmem_limit_bytes=VMEM_LIMIT),
        name="attention_grouped" if grouped else "attention_single",
    )(q5, k, v_t)
    return out.reshape(B, heads * HEAD_V, S)


def _post_kernel(x_ref, oa_ref, om_ref, mod_ref, ga_ref, gb_ref, w_out_ref,
                 ln1g_ref, ln1b_ref, w_gate_ref, w_up_ref, w_down_ref,
                 ln2g_ref, ln2b_ref, y_ref):
    x = x_ref[0]
    mod = mod_ref[0]
    g1, sh2, sc2, g2 = mod[2:3], mod[3:4], mod[4:5], mod[5:6]
    na = _rms_cols(oa_ref[0], ga_ref[...])
    nb = _rms_cols(om_ref[0], gb_ref[...])
    cat_t = jnp.concatenate([na, nb], axis=0).astype(BF16)
    mix = lax.dot_general(cat_t, w_out_ref[...], (((0,), (0,)), ((), ())),
                          preferred_element_type=F32)
    x1 = _layer_norm_rows(ALPHA * x + g1 * mix) * ln1g_ref[...] + ln1b_ref[...]
    u2 = (_layer_norm_rows(x1) * (1.0 + sc2) + sh2).astype(BF16)
    f = None
    for lo, hi in zip(FF_BOUNDS[:-1], FF_BOUNDS[1:]):
        cols = slice(lo, hi)
        gate = jnp.dot(u2, w_gate_ref[:, cols], preferred_element_type=F32)
        up = jnp.dot(u2, w_up_ref[:, cols], preferred_element_type=F32)
        act = (gate * (1.0 / (1.0 + jnp.exp(-gate))) * up).astype(BF16)
        part = jnp.dot(act, w_down_ref[cols, :], preferred_element_type=F32)
        f = part if f is None else f + part
    y_ref[0] = _layer_norm_rows(ALPHA * x1 + g2 * f) * ln2g_ref[...] + ln2b_ref[...]


def _post_attention(x, o_a, o_m, mod, prep):
    B, S, _ = x.shape
    tm = TM_POST
    single = pl.Buffered(1)
    const = lambda shape: pl.BlockSpec(shape, lambda b, i: (0,) * len(shape),
                                       pipeline_mode=single)
    return pl.pallas_call(
        _post_kernel,
        grid=(B, S // tm),
        in_specs=[
            pl.BlockSpec((1, tm, D_MODEL), lambda b, i: (b, i, 0)),
            pl.BlockSpec((1, GQA_WIDTH, tm), lambda b, i: (b, 0, i)),
            pl.BlockSpec((1, MLA_WIDTH, tm), lambda b, i: (b, 0, i)),
            pl.BlockSpec((1, 6, D_MODEL), lambda b, i: (b, 0, 0)),
            const((GQA_WIDTH, 1)),
            const((MLA_WIDTH, 1)),
            const((MIX_WIDTH, D_MODEL)),
            const((1, D_MODEL)),
            const((1, D_MODEL)),
            const((D_MODEL, D_FF)),
            const((D_MODEL, D_FF)),
            const((D_FF, D_MODEL)),
            const((1, D_MODEL)),
            const((1, D_MODEL)),
        ],
        out_specs=pl.BlockSpec((1, tm, D_MODEL), lambda b, i: (b, i, 0)),
        out_shape=jax.ShapeDtypeStruct((B, S, D_MODEL), F32),
        compiler_params=pltpu.CompilerParams(
            dimension_semantics=("parallel", "parallel"),
            vmem_limit_bytes=VMEM_LIMIT),
        name="post_attention",
    )(x, o_a, o_m, mod, prep["ga_col"], prep["gb_col"], prep["w_out"],
      prep["ln1_g"], prep["ln1_b"], prep["w_gate"], prep["w_up"], prep["w_down"],
      prep["ln2_g"], prep["ln2_b"])


def _rope_tables(seq_len, rot_dim):
    rows = seq_len // GRID_W
    row = jnp.repeat(jnp.arange(rows, dtype=F32), GRID_W)
    col = jnp.tile(jnp.arange(GRID_W, dtype=F32), rows)
    n = rot_dim // 4
    inv_freq = 1.0 / jnp.power(ROPE_THETA, jnp.arange(n, dtype=F32) / n)
    ang = jnp.concatenate([inv_freq[:, None] * row[None, :],
                           inv_freq[:, None] * col[None, :]], axis=0)
    return jnp.cos(ang), jnp.sin(ang)


def _trunk(x, mod, prep):
    S = x.shape[1]
    cos_a, sin_a = _rope_tables(S, GQA_HEAD_DIM)
    cos_b, sin_b = _rope_tables(S, MLA_ROPE_DIM)
    ropes = dict(cos_a=cos_a, sin_a=sin_a, cos_b=cos_b, sin_b=sin_b)
    qa_t, k_a, va_t, qm_t, k_m, vm_t = _pre_attention(x, mod, prep, ropes)
    o_a = _attention(qa_t, k_a, va_t, True)
    o_m = _attention(qm_t, k_m, vm_t, False)
    return _post_attention(x, o_a, o_m, mod, prep)


def kernel(x_prompt, x_sample, c_prompt, c_sample, w_ada, b_ada, w_in, gqa_q_norm, gqa_k_norm,
           mla_q_norm, w_uq, mla_kv_norm, w_ukv, grp_norm_a, grp_norm_b, w_out,
           ln1_g, ln1_b, w_gate, w_up, w_down, ln2_g, ln2_b):
    l = 0
    col = lambda v: v.reshape(-1, 1).astype(F32)
    row = lambda v: v.reshape(1, -1).astype(F32)
    prep = dict(
        w_in_t=w_in[l].T.astype(BF16),
        w_uq_t=w_uq[l].T.astype(BF16),
        w_ukv_t=w_ukv[l].T.astype(BF16),
        gq_col=col(gqa_q_norm[l] * (GQA_HEAD_DIM ** -0.5 * LOG2E)),
        gk_col=col(gqa_k_norm[l]),
        gql_col=col(mla_q_norm[l]),
        gkvl_col=col(mla_kv_norm[l]),
        ga_col=col(grp_norm_a[l]),
        gb_col=col(grp_norm_b[l]),
        w_out=w_out[l].astype(BF16),
        ln1_g=row(ln1_g[l]), ln1_b=row(ln1_b[l]),
        w_gate=w_gate[l].astype(BF16),
        w_up=w_up[l].astype(BF16),
        w_down=w_down[l].astype(BF16),
        ln2_g=row(ln2_g[l]), ln2_b=row(ln2_b[l]),
    )
    nb = c_prompt.shape[0]
    mod = _ada_mod(jnp.concatenate([c_prompt, c_sample], axis=0), w_ada[l], b_ada[l])
    mod = mod.reshape(-1, 6, D_MODEL)
    y_prompt = _trunk(x_prompt, mod[:nb], prep)
    y_sample = _trunk(x_sample, mod[nb:], prep)
    return (y_prompt, y_sample)
```

```python
import functools
import math

import jax
import jax.numpy as jnp
from jax import lax
from jax.experimental import pallas as pl
from jax.experimental.pallas import tpu as pltpu

D_MODEL = 1024
DEPTH = 1
GRID_W = 64
ROPE_THETA = 10000.0
NORM_EPS = 1e-6
GQA_HEADS = 8
GQA_KV_HEADS = 2
GQA_GROUP = GQA_HEADS // GQA_KV_HEADS
GQA_HEAD_DIM = 64
MLA_HEADS = 8
MLA_Q_RANK = 256
MLA_KV_RANK = 128
MLA_NOPE_DIM = 64
MLA_ROPE_DIM = 32
MLA_QK_DIM = MLA_NOPE_DIM + MLA_ROPE_DIM
MLA_V_DIM = 64
GQA_WIDTH = GQA_HEADS * GQA_HEAD_DIM
MLA_WIDTH = MLA_HEADS * MLA_V_DIM
MIX_WIDTH = GQA_WIDTH + MLA_WIDTH
GQA_KV_WIDTH = GQA_KV_HEADS * GQA_HEAD_DIM
IN_WIDTH = GQA_WIDTH + 2 * GQA_KV_WIDTH + MLA_Q_RANK + MLA_KV_RANK + MLA_ROPE_DIM
D_FF = -(-8 * D_MODEL // (3 * 256)) * 256
ALPHA = (2.0 * DEPTH) ** 0.25

OFF_QA = 0
OFF_KA = OFF_QA + GQA_WIDTH
OFF_VA = OFF_KA + GQA_KV_WIDTH
OFF_QL = OFF_VA + GQA_KV_WIDTH
OFF_KVL = OFF_QL + MLA_Q_RANK
OFF_KR = OFF_KVL + MLA_KV_RANK

QK_PAD = 128
HEAD_V = 64
ONES_ROWS = 16
HEAD_V_EXT = HEAD_V + ONES_ROWS
LOG2E = math.log2(math.e)

TM_PRE = 1024
TK = 512
CHUNKS_PER_TRIP = 4
N_STREAMS = 16
STREAM_W = 256
TM_POST = 512
MXU_TILE = 256
FF_BOUNDS = (0, (D_FF // MXU_TILE + 1) // 2 * MXU_TILE, D_FF)
VMEM_LIMIT = 56 * 1024 * 1024

F32 = jnp.float32
BF16 = jnp.bfloat16


def _layer_norm_rows(x):
    mu = jnp.mean(x, axis=-1, keepdims=True)
    xc = x - mu
    var = jnp.mean(xc * xc, axis=-1, keepdims=True)
    return xc * lax.rsqrt(var + NORM_EPS)


def _rms_cols(xt, gain_col):
    ms = jnp.mean(xt * xt, axis=0, keepdims=True)
    return xt * lax.rsqrt(ms + NORM_EPS) * gain_col


def _rope_cols(xt, cos_t, sin_t):
    half = xt.shape[0] // 2
    x1, x2 = xt[:half], xt[half:]
    return jnp.concatenate([x1 * cos_t - x2 * sin_t, x1 * sin_t + x2 * cos_t], axis=0)


def _ada_kernel(c_ref, w_ref, b_ref, o_ref):
    c = c_ref[...]
    s = c * (1.0 / (1.0 + jnp.exp(-c)))
    o_ref[...] = jnp.dot(s, w_ref[...], preferred_element_type=F32) + b_ref[...]


def _ada_mod(c, w_ada, b_ada):
    n = c.shape[0]
    cols = 6 * D_MODEL
    return pl.pallas_call(
        _ada_kernel,
        grid=(6,),
        in_specs=[
            pl.BlockSpec((n, D_MODEL), lambda j: (0, 0)),
            pl.BlockSpec((D_MODEL, D_MODEL), lambda j: (0, j)),
            pl.BlockSpec((1, D_MODEL), lambda j: (0, j)),
        ],
        out_specs=pl.BlockSpec((n, D_MODEL), lambda j: (0, j)),
        out_shape=jax.ShapeDtypeStruct((n, cols), F32),
        name="ada_mod",
    )(c, w_ada, b_ada.reshape(1, cols))


def _pre_kernel(x_ref, mod_ref, w_in_ref, w_uq_ref, w_ukv_ref,
                gq_ref, gk_ref, gql_ref, gkvl_ref,
                cosa_ref, sina_ref, cosb_ref, sinb_ref,
                qa_ref, ka_ref, va_ref, qm_ref, km_ref, vm_ref, *, qscale_m):
    x = x_ref[0]
    mod = mod_ref[0]
    sh1, sc1 = mod[0:1], mod[1:2]
    u = (_layer_norm_rows(x) * (1.0 + sc1) + sh1).astype(BF16)
    ht = lax.dot_general(w_in_ref[...], u, (((1,), (1,)), ((), ())),
                         preferred_element_type=F32)
    tm = ht.shape[1]
    n_chunks = tm // TK
    cos_a, sin_a = cosa_ref[...], sina_ref[...]
    cos_b, sin_b = cosb_ref[...], sinb_ref[...]
    zeros_half = jnp.zeros((GQA_HEAD_DIM, tm), F32)
    ones_rows = jnp.where(lax.broadcasted_iota(jnp.int32, (ONES_ROWS, tm), 0) == 0, 1.0, 0.0)

    gq = gq_ref[...]
    for h in range(GQA_HEADS):
        slab = ht[OFF_QA + h * GQA_HEAD_DIM: OFF_QA + (h + 1) * GQA_HEAD_DIM]
        q = _rope_cols(_rms_cols(slab, gq), cos_a, sin_a)
        parts = [zeros_half] * GQA_KV_HEADS
        parts[h // GQA_GROUP] = q
        qa_ref[0, h] = jnp.concatenate(parts, axis=0).astype(BF16)

    gk = gk_ref[...]
    k_parts = []
    for j in range(GQA_KV_HEADS):
        slab = ht[OFF_KA + j * GQA_HEAD_DIM: OFF_KA + (j + 1) * GQA_HEAD_DIM]
        k_parts.append(_rope_cols(_rms_cols(slab, gk), cos_a, sin_a))
        v = ht[OFF_VA + j * GQA_HEAD_DIM: OFF_VA + (j + 1) * GQA_HEAD_DIM]
        v = jnp.concatenate([v, ones_rows], axis=0).astype(BF16)
        for c in range(n_chunks):
            va_ref[0, j, c] = v[:, c * TK:(c + 1) * TK]
    ka_ref[0, 0] = jnp.concatenate(k_parts, axis=0).T.astype(BF16)

    ql = _rms_cols(ht[OFF_QL: OFF_QL + MLA_Q_RANK], gql_ref[...]).astype(BF16)
    qmt = jnp.dot(w_uq_ref[...], ql, preferred_element_type=F32)
    zeros_pad = jnp.zeros((QK_PAD - MLA_QK_DIM, tm), F32)
    for h in range(MLA_HEADS):
        base = h * MLA_QK_DIM
        nope = qmt[base: base + MLA_NOPE_DIM]
        pe = _rope_cols(qmt[base + MLA_NOPE_DIM: base + MLA_QK_DIM], cos_b, sin_b)
        q = jnp.concatenate([nope, pe], axis=0) * qscale_m
        qm_ref[0, h] = jnp.concatenate([q, zeros_pad], axis=0).astype(BF16)

    kvl = _rms_cols(ht[OFF_KVL: OFF_KVL + MLA_KV_RANK], gkvl_ref[...]).astype(BF16)
    kvt = jnp.dot(w_ukv_ref[...], kvl, preferred_element_type=F32)
    k_pe = _rope_cols(ht[OFF_KR: OFF_KR + MLA_ROPE_DIM], cos_b, sin_b)
    for h in range(MLA_HEADS):
        base = h * (MLA_NOPE_DIM + MLA_V_DIM)
        k_nope = kvt[base: base + MLA_NOPE_DIM]
        v = kvt[base + MLA_NOPE_DIM: base + MLA_NOPE_DIM + MLA_V_DIM]
        v = jnp.concatenate([v, ones_rows], axis=0).astype(BF16)
        for c in range(n_chunks):
            vm_ref[0, h, c] = v[:, c * TK:(c + 1) * TK]
        kt = jnp.concatenate([k_nope, k_pe, zeros_pad], axis=0)
        km_ref[0, h] = kt.T.astype(BF16)


def _pre_attention(x, mod, prep, ropes):
    B, S, _ = x.shape
    tm = TM_PRE
    nt = S // tm
    cpt = tm // TK
    nck = S // TK
    const = lambda shape: pl.BlockSpec(shape, lambda b, i: (0,) * len(shape))
    rope_a = pl.BlockSpec((GQA_HEAD_DIM // 2, tm), lambda b, i: (0, i))
    rope_b = pl.BlockSpec((MLA_ROPE_DIM // 2, tm), lambda b, i: (0, i))
    out_shape = (
        jax.ShapeDtypeStruct((B, GQA_HEADS, QK_PAD, S), BF16),
        jax.ShapeDtypeStruct((B, 1, S, QK_PAD), BF16),
        jax.ShapeDtypeStruct((B, GQA_KV_HEADS, nck, HEAD_V_EXT, TK), BF16),
        jax.ShapeDtypeStruct((B, MLA_HEADS, QK_PAD, S), BF16),
        jax.ShapeDtypeStruct((B, MLA_HEADS, S, QK_PAD), BF16),
        jax.ShapeDtypeStruct((B, MLA_HEADS, nck, HEAD_V_EXT, TK), BF16),
    )
    out_specs = (
        pl.BlockSpec((1, GQA_HEADS, QK_PAD, tm), lambda b, i: (b, 0, 0, i)),
        pl.BlockSpec((1, 1, tm, QK_PAD), lambda b, i: (b, 0, i, 0)),
        pl.BlockSpec((1, GQA_KV_HEADS, cpt, HEAD_V_EXT, TK), lambda b, i: (b, 0, i, 0, 0)),
        pl.BlockSpec((1, MLA_HEADS, QK_PAD, tm), lambda b, i: (b, 0, 0, i)),
        pl.BlockSpec((1, MLA_HEADS, tm, QK_PAD), lambda b, i: (b, 0, i, 0)),
        pl.BlockSpec((1, MLA_HEADS, cpt, HEAD_V_EXT, TK), lambda b, i: (b, 0, i, 0, 0)),
    )
    return pl.pallas_call(
        functools.partial(_pre_kernel, qscale_m=MLA_QK_DIM ** -0.5 * LOG2E),
        grid=(B, nt),
        in_specs=[
            pl.BlockSpec((1, tm, D_MODEL), lambda b, i: (b, i, 0)),
            pl.BlockSpec((1, 6, D_MODEL), lambda b, i: (b, 0, 0)),
            const((IN_WIDTH, D_MODEL)),
            const((MLA_HEADS * MLA_QK_DIM, MLA_Q_RANK)),
            const((MLA_HEADS * (MLA_NOPE_DIM + MLA_V_DIM), MLA_KV_RANK)),
            const((GQA_HEAD_DIM, 1)),
            const((GQA_HEAD_DIM, 1)),
            const((MLA_Q_RANK, 1)),
            const((MLA_KV_RANK, 1)),
            rope_a, rope_a, rope_b, rope_b,
        ],
        out_specs=out_specs,
        out_shape=out_shape,
        compiler_params=pltpu.CompilerParams(
            dimension_semantics=("parallel", "parallel"),
            vmem_limit_bytes=VMEM_LIMIT),
        name="pre_attention",
    )(x, mod, prep["w_in_t"], prep["w_uq_t"], prep["w_ukv_t"],
      prep["gq_col"], prep["gk_col"], prep["gql_col"], prep["gkvl_col"],
      ropes["cos_a"], ropes["sin_a"], ropes["cos_b"], ropes["sin_b"])


def _attn_kernel(q_ref, k_ref, v_ref, o_ref, s0_ref, s1_ref, cmax0_ref, cmax1_ref, acc_ref,
                 *, n_chunks):
    group = q_ref.shape[2]
    n_streams = acc_ref.shape[0]

    def lanes(i):
        t = i // group
        return slice(t * STREAM_W, (t + 1) * STREAM_W)

    def q_stream(i):
        return q_ref[0, 0, i % group, :, lanes(i)]

    def load_k(c):
        off = pl.multiple_of(c * TK, TK)
        return k_ref[0, 0, pl.ds(off, TK), :]

    def qk(kc, slot, i):
        s_ref, cmax_ref = slot
        s = jnp.dot(kc, q_stream(i), preferred_element_type=F32)
        s_ref[i] = s
        cmax_ref[i] = jnp.max(s, axis=0, keepdims=True)

    def scores(c, slot):
        kc = load_k(c)
        for i in range(n_streams):
            qk(kc, slot, i)

    def softmax(slot, i, m_old):
        s_ref, cmax_ref = slot
        m_new = jnp.maximum(m_old, cmax_ref[i])
        p = jnp.exp2(s_ref[i] - m_new).astype(BF16)
        return m_new, jnp.exp2(m_old - m_new), p

    def pv(c, i, alpha, p):
        acc_ref[i] = alpha * acc_ref[i] + jnp.dot(v_ref[0, 0, c], p, preferred_element_type=F32)

    slots = ((s0_ref, cmax0_ref), (s1_ref, cmax1_ref))

    def trip(c0, ms, prefetch_next_trip):
        ms = list(ms)
        for u in range(CHUNKS_PER_TRIP):
            c = c0 + u
            produce = u + 1 < CHUNKS_PER_TRIP or prefetch_next_trip
            if produce:
                kc = load_k(c + 1)
            for i in range(n_streams):
                if produce:
                    qk(kc, slots[(u + 1) % 2], i)
                ms[i], alpha, p = softmax(slots[u % 2], i, ms[i])
                pv(c, i, alpha, p)
        return tuple(ms)

    acc_ref[...] = jnp.zeros(acc_ref.shape, F32)
    ms = tuple(jnp.full((1, STREAM_W), -jnp.inf, F32) for _ in range(n_streams))
    scores(0, slots[0])
    ms = lax.fori_loop(0, n_chunks // CHUNKS_PER_TRIP - 1,
                       lambda t, ms: trip(t * CHUNKS_PER_TRIP, ms, True), ms)
    trip(n_chunks - CHUNKS_PER_TRIP, ms, False)
    for i in range(n_streams):
        acc = acc_ref[i]
        o = acc[:HEAD_V] * (1.0 / acc[HEAD_V:HEAD_V + 1])
        o_ref[0, 0, i % group, :, lanes(i)] = o


def _attention(q_t, k, v_t, grouped):
    B, heads, _, S = q_t.shape
    n_chunks = S // TK
    assert n_chunks % CHUNKS_PER_TRIP == 0
    group = GQA_GROUP if grouped else 1
    kv = heads // group
    n_streams = min(N_STREAMS, S // STREAM_W * group)
    tq = n_streams // group * STREAM_W
    q5 = q_t.reshape(B, kv, group, QK_PAD, S)
    q_spec = pl.BlockSpec((1, 1, group, QK_PAD, tq), lambda b, j, i: (b, j, 0, 0, i))
    k_map = (lambda b, j, i: (b, 0, 0, 0)) if grouped else (lambda b, j, i: (b, j, 0, 0))
    o_shape = (B, kv, group, HEAD_V, S)
    o_spec = pl.BlockSpec((1, 1, group, HEAD_V, tq), lambda b, j, i: (b, j, 0, 0, i))
    nq = S // tq
    out = pl.pallas_call(
        functools.partial(_attn_kernel, n_chunks=n_chunks),
        grid=(B, kv, nq),
        in_specs=[
            q_spec,
            pl.BlockSpec((1, 1, S, QK_PAD), k_map),
            pl.BlockSpec((1, 1, n_chunks, HEAD_V_EXT, TK), lambda b, j, i: (b, j, 0, 0, 0)),
        ],
        out_specs=o_spec,
        out_shape=jax.ShapeDtypeStruct(o_shape, F32),
        scratch_shapes=[
            pltpu.VMEM((n_streams, TK, STREAM_W), F32),
            pltpu.VMEM((n_streams, TK, STREAM_W), F32),
            pltpu.VMEM((n_streams, 1, STREAM_W), F32),
            pltpu.VMEM((n_streams, 1, STREAM_W), F32),
            pltpu.VMEM((n_streams, HEAD_V_EXT, STREAM_W), F32),
        ],
        compiler_params=pltpu.CompilerParams(
            dimension_semantics=("parallel", "parallel", "parallel"),
            vmem_limit_bytes=VMEM_LIMIT),
        name="attention_grouped" if grouped else "attention_single",
    )(q5, k, v_t)
    return out.reshape(B, heads * HEAD_V, S)


def _post_kernel(x_ref, oa_ref, om_ref, mod_ref, ga_ref, gb_ref, w_out_ref,
                 ln1g_ref, ln1b_ref, w_gate_ref, w_up_ref, w_down_ref,
                 ln2g_ref, ln2b_ref, y_ref):
    x = x_ref[0]
    mod = mod_ref[0]
    g1, sh2, sc2, g2 = mod[2:3], mod[3:4], mod[4:5], mod[5:6]
    na = _rms_cols(oa_ref[0], ga_ref[...])
    nb = _rms_cols(om_ref[0], gb_ref[...])
    cat_t = jnp.concatenate([na, nb], axis=0).astype(BF16)
    mix = lax.dot_general(cat_t, w_out_ref[...], (((0,), (0,)), ((), ())),
                          preferred_element_type=F32)
    x1 = _layer_norm_rows(ALPHA * x + g1 * mix) * ln1g_ref[...] + ln1b_ref[...]
    u2 = (_layer_norm_rows(x1) * (1.0 + sc2) + sh2).astype(BF16)
    f = None
    for lo, hi in zip(FF_BOUNDS[:-1], FF_BOUNDS[1:]):
        cols = slice(lo, hi)
        gate = jnp.dot(u2, w_gate_ref[:, cols], preferred_element_type=F32)
        up = jnp.dot(u2, w_up_ref[:, cols], preferred_element_type=F32)
        act = (gate * (1.0 / (1.0 + jnp.exp(-gate))) * up).astype(BF16)
        part = jnp.dot(act, w_down_ref[cols, :], preferred_element_type=F32)
        f = part if f is None else f + part
    y_ref[0] = _layer_norm_rows(ALPHA * x1 + g2 * f) * ln2g_ref[...] + ln2b_ref[...]


def _post_attention(x, o_a, o_m, mod, prep):
    B, S, _ = x.shape
    tm = TM_POST
    single = pl.Buffered(1)
    const = lambda shape: pl.BlockSpec(shape, lambda b, i: (0,) * len(shape),
                                       pipeline_mode=single)
    return pl.pallas_call(
        _post_kernel,
        grid=(B, S // tm),
        in_specs=[
            pl.BlockSpec((1, tm, D_MODEL), lambda b, i: (b, i, 0)),
            pl.BlockSpec((1, GQA_WIDTH, tm), lambda b, i: (b, 0, i)),
            pl.BlockSpec((1, MLA_WIDTH, tm), lambda b, i: (b, 0, i)),
            pl.BlockSpec((1, 6, D_MODEL), lambda b, i: (b, 0, 0)),
            const((GQA_WIDTH, 1)),
            const((MLA_WIDTH, 1)),
            const((MIX_WIDTH, D_MODEL)),
            const((1, D_MODEL)),
            const((1, D_MODEL)),
            const((D_MODEL, D_FF)),
            const((D_MODEL, D_FF)),
            const((D_FF, D_MODEL)),
            const((1, D_MODEL)),
            const((1, D_MODEL)),
        ],
        out_specs=pl.BlockSpec((1, tm, D_MODEL), lambda b, i: (b, i, 0)),
        out_shape=jax.ShapeDtypeStruct((B, S, D_MODEL), F32),
        compiler_params=pltpu.CompilerParams(
            dimension_semantics=("parallel", "parallel"),
            vmem_limit_bytes=VMEM_LIMIT),
        name="post_attention",
    )(x, o_a, o_m, mod, prep["ga_col"], prep["gb_col"], prep["w_out"],
      prep["ln1_g"], prep["ln1_b"], prep["w_gate"], prep["w_up"], prep["w_down"],
      prep["ln2_g"], prep["ln2_b"])


def _rope_tables(seq_len, rot_dim):
    rows = seq_len // GRID_W
    row = jnp.repeat(jnp.arange(rows, dtype=F32), GRID_W)
    col = jnp.tile(jnp.arange(GRID_W, dtype=F32), rows)
    n = rot_dim // 4
    inv_freq = 1.0 / jnp.power(ROPE_THETA, jnp.arange(n, dtype=F32) / n)
    ang = jnp.concatenate([inv_freq[:, None] * row[None, :],
                           inv_freq[:, None] * col[None, :]], axis=0)
    return jnp.cos(ang), jnp.sin(ang)


def _trunk(x, mod, prep):
    S = x.shape[1]
    cos_a, sin_a = _rope_tables(S, GQA_HEAD_DIM)
    cos_b, sin_b = _rope_tables(S, MLA_ROPE_DIM)
    ropes = dict(cos_a=cos_a, sin_a=sin_a, cos_b=cos_b, sin_b=sin_b)
    qa_t, k_a, va_t, qm_t, k_m, vm_t = _pre_attention(x, mod, prep, ropes)
    o_a = _attention(qa_t, k_a, va_t, True)
    o_m = _attention(qm_t, k_m, vm_t, False)
    return _post_attention(x, o_a, o_m, mod, prep)


def kernel(x_prompt, x_sample, c_prompt, c_sample, w_ada, b_ada, w_in, gqa_q_norm, gqa_k_norm,
           mla_q_norm, w_uq, mla_kv_norm, w_ukv, grp_norm_a, grp_norm_b, w_out,
           ln1_g, ln1_b, w_gate, w_up, w_down, ln2_g, ln2_b):
    l = 0
    col = lambda v: v.reshape(-1, 1).astype(F32)
    row = lambda v: v.reshape(1, -1).astype(F32)
    prep = dict(
        w_in_t=w_in[l].T.astype(BF16),
        w_uq_t=w_uq[l].T.astype(BF16),
        w_ukv_t=w_ukv[l].T.astype(BF16),
        gq_col=col(gqa_q_norm[l] * (GQA_HEAD_DIM ** -0.5 * LOG2E)),
        gk_col=col(gqa_k_norm[l]),
        gql_col=col(mla_q_norm[l]),
        gkvl_col=col(mla_kv_norm[l]),
        ga_col=col(grp_norm_a[l]),
        gb_col=col(grp_norm_b[l]),
        w_out=w_out[l].astype(BF16),
        ln1_g=row(ln1_g[l]), ln1_b=row(ln1_b[l]),
        w_gate=w_gate[l].astype(BF16),
        w_up=w_up[l].astype(BF16),
        w_down=w_down[l].astype(BF16),
        ln2_g=row(ln2_g[l]), ln2_b=row(ln2_b[l]),
    )
    nb = c_prompt.shape[0]
    mod = _ada_mod(jnp.concatenate([c_prompt, c_sample], axis=0), w_ada[l], b_ada[l])
    mod = mod.reshape(-1, 6, D_MODEL)
    y_prompt = _trunk(x_prompt, mod[:nb], prep)
    y_sample = _trunk(x_sample, mod[nb:], prep)
    return (y_prompt, y_sample)
```
